```python
import jax, jax.numpy as jnp
from jax import lax
import numpy as np

D_MODEL = 4096
BATCH = 1
SEQ = 8192
DEPTH = 4
DEC_BATCH = 8
DEC_SEQ = 16
PAST_LEN = 1024

CHUNK = 64
HEAD_DIM = 128
SB_HEADS = 16
FOX_HEADS = 16
SB_WIDTH = SB_HEADS * HEAD_DIM
FOX_WIDTH = FOX_HEADS * HEAD_DIM
D_FF = ((8 * D_MODEL // 3 + 255) // 256) * 256
QBLOCK = 128
EPS = 1e-6
_SPLITS = [int(s) for s in np.cumsum([SB_WIDTH, SB_WIDTH, SB_WIDTH, FOX_WIDTH, FOX_WIDTH, FOX_WIDTH])]
N_IN = _SPLITS[-1] + FOX_HEADS

kernel_name = "stickbreak_fox_gated_hybrid_step"


def rms_norm(x, g):
    xf = x.astype(jnp.float32)
    y = xf * lax.rsqrt(jnp.mean(xf * xf, axis=-1, keepdims=True) + EPS)
    return (y * g.astype(jnp.float32)).astype(x.dtype)


def project_inputs(xn, w_in, b_forget, q_gain, k_gain):
    B, T, _ = xn.shape
    h = xn @ w_in
    parts = jnp.split(h, _SPLITS, axis=-1)
    qa, ka, va = [p.reshape(B, T, SB_HEADS, HEAD_DIM) for p in parts[0:3]]
    qb, kb, vb = [p.reshape(B, T, FOX_HEADS, HEAD_DIM) for p in parts[3:6]]
    logf = jax.nn.log_sigmoid(parts[6].astype(jnp.float32) + b_forget.astype(jnp.float32))
    qb = rms_norm(qb, q_gain)
    kb = rms_norm(kb, k_gain)
    return qa, ka, va, qb, kb, vb, logf


def stick_breaking(q, k, v, q_pos, k_pos):
    z = jnp.einsum('bqhd,bkhd->bhqk', q, k).astype(jnp.float32) * (HEAD_DIM ** -0.5)
    valid = k_pos[None, :] < q_pos[:, None]
    log_not = jnp.where(valid, jax.nn.log_sigmoid(-z), 0.0)
    between = lax.cumsum(log_not, axis=3, reverse=True) - log_not
    a = jnp.where(valid, jnp.exp(jax.nn.log_sigmoid(z) + between), 0.0)
    return jnp.einsum('bhqk,bkhd->bqhd', a.astype(v.dtype), v)


def forgetting_attn(q, k, v, cum_q, cum_k, q_pos, k_pos):
    s = jnp.einsum('bqhd,bkhd->bhqk', q, k).astype(jnp.float32) * (HEAD_DIM ** -0.5)
    bias = jnp.transpose(cum_q, (0, 2, 1))[:, :, :, None] - jnp.transpose(cum_k, (0, 2, 1))[:, :, None, :]
    valid = k_pos[None, :] <= q_pos[:, None]
    p = jax.nn.softmax(jnp.where(valid, s + bias, -jnp.inf), axis=-1)
    return jnp.einsum('bhqk,bkhd->bqhd', p.astype(v.dtype), v)


def _blocks(a):
    B, T = a.shape[:2]
    a = a.reshape((B, T // QBLOCK, QBLOCK) + a.shape[2:])
    return jnp.moveaxis(a, 1, 0)


def _unblocks(a):
    a = jnp.moveaxis(a, 0, 1)
    return a.reshape((a.shape[0], a.shape[1] * a.shape[2]) + a.shape[3:])


def prompt_mixers(qa, ka, va, qb, kb, vb, logf):
    T = qa.shape[1]
    pos = jnp.arange(T)
    pos_b = pos.reshape(T // QBLOCK, QBLOCK)
    cum = jnp.cumsum(logf, axis=1)
    o_a = _unblocks(lax.map(lambda a: stick_breaking(a[0], ka, va, a[1], pos), (_blocks(qa), pos_b)))
    o_b = _unblocks(lax.map(lambda a: forgetting_attn(a[0], kb, vb, a[1], cum, a[2], pos),
                            (_blocks(qb), _blocks(cum), pos_b)))
    return o_a, o_b


def sample_mixers(qa, ka, va, qb, kb, vb, logf, c_sb_k, c_sb_v, c_fx_k, c_fx_v, c_fx_logf):
    Tn = qa.shape[1]
    Tp = c_sb_k.shape[1]
    q_pos = Tp + jnp.arange(Tn)
    k_pos = jnp.arange(Tp + Tn)
    k_a = jnp.concatenate([c_sb_k, ka], axis=1)
    v_a = jnp.concatenate([c_sb_v, va], axis=1)
    k_b = jnp.concatenate([c_fx_k, kb], axis=1)
    v_b = jnp.concatenate([c_fx_v, vb], axis=1)
    cum = jnp.cumsum(jnp.concatenate([c_fx_logf.astype(jnp.float32), logf], axis=1), axis=1)
    o_a = stick_breaking(qa, k_a, v_a, q_pos, k_pos)
    o_b = forgetting_attn(qb, k_b, v_b, cum[:, Tp:], cum, q_pos, k_pos)
    return o_a, o_b


def merge_and_ffn(x, xn, o_a, o_b, w_proj_a, w_proj_b, w_gate_br, b_gate_br, w_out,
                  norm_ffn, w_ffn_gate, w_ffn_up, w_ffn_down):
    B, T, _ = x.shape
    br_a = o_a.reshape(B, T, SB_WIDTH) @ w_proj_a
    br_b = o_b.reshape(B, T, FOX_WIDTH) @ w_proj_b
    g = jax.nn.sigmoid((xn @ w_gate_br + b_gate_br).astype(jnp.float32)).astype(x.dtype)
    g_a, g_b = jnp.split(g, 2, axis=-1)
    h = x + (g_a * br_a + g_b * br_b) @ w_out
    hn = rms_norm(h, norm_ffn)
    return h + (jax.nn.silu(hn @ w_ffn_gate) * (hn @ w_ffn_up)) @ w_ffn_down


def setup_inputs(seed: int = 0) -> dict:
    key = jax.random.key(seed)
    ks = jax.random.split(key, 24)
    f32 = jnp.float32
    nrm = lambda k, shape, scale: jax.random.normal(k, shape, f32) * scale
    cache_kv = (DEPTH, DEC_BATCH, PAST_LEN, SB_HEADS, HEAD_DIM)
    cache_kv_b = (DEPTH, DEC_BATCH, PAST_LEN, FOX_HEADS, HEAD_DIM)
    return {
        "x_prompt": nrm(ks[0], (BATCH, SEQ, D_MODEL), 1.0),
        "x_sample": nrm(ks[1], (DEC_BATCH, DEC_SEQ, D_MODEL), 1.0),
        "cache_sb_k": nrm(ks[2], cache_kv, 1.0),
        "cache_sb_v": nrm(ks[3], cache_kv, 1.0),
        "cache_fox_k": nrm(ks[4], cache_kv_b, 1.0),
        "cache_fox_v": nrm(ks[5], cache_kv_b, 1.0),
        "cache_fox_logf": jax.nn.log_sigmoid(3.0 + nrm(ks[6], (DEPTH, DEC_BATCH, PAST_LEN, FOX_HEADS), 1.0)),
        "norm_attn": 1.0 + nrm(ks[7], (DEPTH, D_MODEL), 0.02),
        "w_in": nrm(ks[8], (DEPTH, D_MODEL, N_IN), D_MODEL ** -0.5),
        "b_forget": jax.random.uniform(ks[9], (DEPTH, FOX_HEADS), f32, 1.0, 6.0),
        "q_norm": 1.0 + nrm(ks[10], (DEPTH, HEAD_DIM), 0.02),
        "k_norm": 1.0 + nrm(ks[11], (DEPTH, HEAD_DIM), 0.02),
        "w_proj_a": nrm(ks[12], (DEPTH, SB_WIDTH, D_MODEL), SB_WIDTH ** -0.5),
        "w_proj_b": nrm(ks[13], (DEPTH, FOX_WIDTH, D_MODEL), FOX_WIDTH ** -0.5),
        "w_gate_br": nrm(ks[14], (DEPTH, D_MODEL, 2 * D_MODEL), D_MODEL ** -0.5),
        "b_gate_br": nrm(ks[15], (DEPTH, 2 * D_MODEL), 0.02),
        "w_out": nrm(ks[16], (DEPTH, D_MODEL, D_MODEL), D_MODEL ** -0.5),
        "norm_ffn": 1.0 + nrm(ks[17], (DEPTH, D_MODEL), 0.02),
        "w_ffn_gate": nrm(ks[18], (DEPTH, D_MODEL, D_FF), D_MODEL ** -0.5),
        "w_ffn_up": nrm(ks[19], (DEPTH, D_MODEL, D_FF), D_MODEL ** -0.5),
        "w_ffn_down": nrm(ks[20], (DEPTH, D_FF, D_MODEL), D_FF ** -0.5),
    }


def reference(x_prompt, x_sample, cache_sb_k, cache_sb_v, cache_fox_k, cache_fox_v, cache_fox_logf,
              norm_attn, w_in, b_forget, q_norm, k_norm, w_proj_a, w_proj_b, w_gate_br, b_gate_br,
              w_out, norm_ffn, w_ffn_gate, w_ffn_up, w_ffn_down):
    xp, xs = x_prompt, x_sample
    p_sb_k, p_sb_v, p_fx_k, p_fx_v, p_fx_lf = [], [], [], [], []
    s_sb_k, s_sb_v, s_fx_k, s_fx_v, s_fx_lf = [], [], [], [], []
    for l in range(DEPTH):
        xn = rms_norm(xp, norm_attn[l])
        qa, ka, va, qb, kb, vb, logf = project_inputs(xn, w_in[l], b_forget[l], q_norm[l], k_norm[l])
        o_a, o_b = prompt_mixers(qa, ka, va, qb, kb, vb, logf)
        xp = merge_and_ffn(xp, xn, o_a, o_b, w_proj_a[l], w_proj_b[l], w_gate_br[l], b_gate_br[l],
                           w_out[l], norm_ffn[l], w_ffn_gate[l], w_ffn_up[l], w_ffn_down[l])
        p_sb_k.append(ka); p_sb_v.append(va); p_fx_k.append(kb); p_fx_v.append(vb)
        p_fx_lf.append(logf.astype(xp.dtype))
        xn = rms_norm(xs, norm_attn[l])
        qa, ka, va, qb, kb, vb, logf = project_inputs(xn, w_in[l], b_forget[l], q_norm[l], k_norm[l])
        o_a, o_b = sample_mixers(qa, ka, va, qb, kb, vb, logf, cache_sb_k[l], cache_sb_v[l],
                                 cache_fox_k[l], cache_fox_v[l], cache_fox_logf[l])
        xs = merge_and_ffn(xs, xn, o_a, o_b, w_proj_a[l], w_proj_b[l], w_gate_br[l], b_gate_br[l],
                           w_out[l], norm_ffn[l], w_ffn_gate[l], w_ffn_up[l], w_ffn_down[l])
        s_sb_k.append(ka); s_sb_v.append(va); s_fx_k.append(kb); s_fx_v.append(vb)
        s_fx_lf.append(logf.astype(xs.dtype))
    return (xp, xs,
            jnp.stack(p_sb_k), jnp.stack(p_sb_v), jnp.stack(p_fx_k), jnp.stack(p_fx_v), jnp.stack(p_fx_lf),
            jnp.stack(s_sb_k), jnp.stack(s_sb_v), jnp.stack(s_fx_k), jnp.stack(s_fx_v), jnp.stack(s_fx_lf))
```

```python
import functools

import jax
import jax.numpy as jnp
from jax import lax
from jax.experimental import pallas as pl
from jax.experimental.pallas import tpu as pltpu

EPS = 1e-6
HEAD_DIM = 128
LANES = 128
BF16_ROWS = 16
VMEM_LIMIT = 56 * 1024 * 1024
F32 = jnp.float32
BF16 = jnp.bfloat16


def _pick(n, target, mult):
    best = None
    for d in range(mult, min(n, target) + 1, mult):
        if n % d == 0:
            best = d
    return best if best is not None else n


def _params(*sem):
    return pltpu.CompilerParams(dimension_semantics=sem, vmem_limit_bytes=VMEM_LIMIT)


def _dot(a, b):
    return jnp.dot(a, b, preferred_element_type=F32)


def _dot_nt(a, b):
    return lax.dot_general(a, b, (((1,), (1,)), ((), ())), preferred_element_type=F32)


def _log_sigmoid(x):
    return jnp.minimum(x, 0.0) - jnp.log1p(jnp.exp(-jnp.abs(x)))


def _rmsnorm_kernel(x_ref, g_ref, o_ref):
    x = x_ref[...]
    y = x * lax.rsqrt(jnp.mean(x * x, axis=-1, keepdims=True) + EPS)
    o_ref[...] = (y * g_ref[...]).astype(o_ref.dtype)


def _rmsnorm(x, g):
    rows, d = x.shape
    tm = _pick(rows, 640, BF16_ROWS)
    return pl.pallas_call(
        _rmsnorm_kernel,
        grid=(rows // tm,),
        in_specs=[pl.BlockSpec((tm, d), lambda i: (i, 0)), pl.BlockSpec((1, d), lambda i: (0, 0))],
        out_specs=pl.BlockSpec((tm, d), lambda i: (i, 0)),
        out_shape=jax.ShapeDtypeStruct((rows, d), BF16),
        compiler_params=_params("parallel"),
        name="rmsnorm",
    )(x, g.reshape(1, d))


def _head_rmsnorm(acc, gain):
    outs = []
    for c in range(acc.shape[1] // HEAD_DIM):
        blk = acc[:, c * HEAD_DIM:(c + 1) * HEAD_DIM]
        y = blk * lax.rsqrt(jnp.mean(blk * blk, axis=-1, keepdims=True) + EPS)
        outs.append(y * gain)
    return outs[0] if len(outs) == 1 else jnp.concatenate(outs, axis=1)


def _proj_q_kernel(xn_ref, w_ref, gain_ref, o_ref, *, nb, scale):
    acc = _dot(xn_ref[...], w_ref[...])

    @pl.when(pl.program_id(1) < nb)
    def _():
        o_ref[...] = (acc * scale).astype(o_ref.dtype)

    @pl.when(pl.program_id(1) >= nb)
    def _():
        o_ref[...] = (_head_rmsnorm(acc, gain_ref[...]) * scale).astype(o_ref.dtype)


def _proj_kv_kernel(xn_ref, w_ref, gain_ref, of_ref, ob_ref, *, nb):
    acc = _dot(xn_ref[...], w_ref[...])
    j = pl.program_id(1)
    is_kfox = jnp.logical_and(j >= 2 * nb, j < 3 * nb)

    @pl.when(jnp.logical_not(is_kfox))
    def _():
        of_ref[...] = acc
        ob_ref[...] = acc.astype(ob_ref.dtype)

    @pl.when(is_kfox)
    def _():
        y = _head_rmsnorm(acc, gain_ref[...])
        of_ref[...] = y
        ob_ref[...] = y.astype(ob_ref.dtype)


def _project_qkv(xn, w_main, q_gain, k_gain, width):
    rows, d = xn.shape
    tm = _pick(rows, 1040, BF16_ROWS)
    tn = _pick(width, 512, LANES)
    nb = width // tn
    scale = HEAD_DIM ** -0.5
    x_spec = pl.BlockSpec((tm, d), lambda i, j: (i, 0))
    g_spec = pl.BlockSpec((1, HEAD_DIM), lambda i, j: (0, 0))
    o_spec = pl.BlockSpec((tm, tn), lambda i, j: (i, j))

    qq = pl.pallas_call(
        functools.partial(_proj_q_kernel, nb=nb, scale=scale),
        grid=(rows // tm, 2 * nb),
        in_specs=[x_spec,
                  pl.BlockSpec((d, tn), lambda i, j: (0, jnp.where(j < nb, j, j + 2 * nb))),
                  g_spec],
        out_specs=o_spec,
        out_shape=jax.ShapeDtypeStruct((rows, 2 * width), BF16),
        compiler_params=_params("parallel", "arbitrary"),
        name="proj_q",
    )(xn, w_main, q_gain.reshape(1, HEAD_DIM))

    kvf, kvb = pl.pallas_call(
        functools.partial(_proj_kv_kernel, nb=nb),
        grid=(rows // tm, 4 * nb),
        in_specs=[x_spec,
                  pl.BlockSpec((d, tn), lambda i, j: (0, jnp.where(j < 2 * nb, j + nb, j + 2 * nb))),
                  g_spec],
        out_specs=[o_spec, o_spec],
        out_shape=[jax.ShapeDtypeStruct((rows, 4 * width), F32),
                   jax.ShapeDtypeStruct((rows, 4 * width), BF16)],
        compiler_params=_params("parallel", "arbitrary"),
        name="proj_kv",
    )(xn, w_main, k_gain.reshape(1, HEAD_DIM))
    return qq, kvf, kvb


def _logf_kernel(wf_ref, xn_ref, b_ref, o_ref):
    z = _dot_nt(wf_ref[...], xn_ref[...]) + b_ref[...]
    o_ref[...] = _log_sigmoid(z)


def _log_forget(xn, wf_t, b_forget):
    rows, d = xn.shape
    heads = wf_t.shape[0]
    tt = _pick(rows, 1664, LANES)
    return pl.pallas_call(
        _logf_kernel,
        grid=(rows // tt,),
        in_specs=[pl.BlockSpec((heads, d), lambda i: (0, 0)),
                  pl.BlockSpec((tt, d), lambda i: (i, 0)),
                  pl.BlockSpec((heads, 1), lambda i: (0, 0))],
        out_specs=pl.BlockSpec((heads, tt), lambda i: (0, i)),
        out_shape=jax.ShapeDtypeStruct((heads, rows), F32),
        compiler_params=_params("parallel"),
        name="log_forget",
    )(wf_t, xn, b_forget.reshape(heads, 1))


def _split3(x):
    hi = x.astype(BF16)
    r1 = x - hi.astype(F32)
    mid = r1.astype(BF16)
    lo = (r1 - mid.astype(F32)).astype(BF16)
    return hi, mid, lo


def _cumsum_kernel(x_ref, o_ref):
    rows, length = x_ref.shape
    r = lax.broadcasted_iota(jnp.int32, (LANES, LANES), 0)
    c = lax.broadcasted_iota(jnp.int32, (LANES, LANES), 1)
    tri = jnp.where(r <= c, 1.0, 0.0).astype(BF16)

    def body(n, carry):
        off = pl.multiple_of(n * LANES, LANES)
        hi, mid, lo = _split3(x_ref[:, pl.ds(off, LANES)])
        y = (_dot(hi, tri) + _dot(mid, tri)) + _dot(lo, tri) + carry
        o_ref[:, pl.ds(off, LANES)] = y
        return y[:, LANES - 1:LANES]

    lax.fori_loop(0, length // LANES, body, jnp.zeros((rows, 1), F32))


def _cumsum_lanes(x):
    return pl.pallas_call(
        _cumsum_kernel,
        out_shape=jax.ShapeDtypeStruct(x.shape, F32),
        compiler_params=pltpu.CompilerParams(vmem_limit_bytes=VMEM_LIMIT),
        name="cumsum",
    )(x)


def _tri_masks(blk):
    row = lax.broadcasted_iota(jnp.int32, (blk, blk), 0)
    col = lax.broadcasted_iota(jnp.int32, (blk, blk), 1)
    return row, col


def _suffix_ones(blk):
    r = lax.broadcasted_iota(jnp.int32, (2 * blk, blk), 0)
    c = lax.broadcasted_iota(jnp.int32, (2 * blk, blk), 1)
    r = jnp.where(r >= blk, r - blk, r)
    return jnp.where(r >= c, 1.0, 0.0).astype(BF16)


def _sb_block(q, k, v, suffix, carry, valid):
    z = _dot_nt(q, k)
    log_not = _log_sigmoid(-z)
    if valid is not None:
        log_not = jnp.where(valid, log_not, 0.0)
    hi = log_not.astype(BF16)
    lo = (log_not - hi.astype(F32)).astype(BF16)
    tail = _dot(jnp.concatenate([hi, lo], axis=1), suffix) + carry
    a = jnp.exp(z + tail)
    if valid is not None:
        a = jnp.where(valid, a, 0.0)
    return _dot(a.astype(BF16), v), tail[:, 0:1]


def _sb_prompt_kernel(q_ref, k_ref, v_ref, o_ref, *, blk):
    i = pl.program_id(1)
    q = q_ref[...]
    suffix = _suffix_ones(blk)
    row, col = _tri_masks(blk)

    def kv(j):
        off = pl.multiple_of(j * blk, blk)
        return k_ref[pl.ds(off, blk), :], v_ref[pl.ds(off, blk), :]

    k, v = kv(i)
    acc, carry = _sb_block(q, k, v, suffix, jnp.zeros((blk, 1), F32), col < row)

    def body(n, state):
        acc, carry = state
        k, v = kv(i - 1 - n)
        out, carry = _sb_block(q, k, v, suffix, carry, None)
        return acc + out, carry

    acc, _ = lax.fori_loop(0, i, body, (acc, carry))
    o_ref[...] = acc.astype(o_ref.dtype)


def _fox_prompt_kernel(q_ref, k_ref, v_ref, ck_ref, o_ref, *, blk):
    i = pl.program_id(1)
    q = q_ref[...]
    row, col = _tri_masks(blk)

    def scores(j):
        off = pl.multiple_of(j * blk, blk)
        s = _dot_nt(q, k_ref[pl.ds(off, blk), :]) - ck_ref[:, pl.ds(off, blk)]
        return s, v_ref[pl.ds(off, blk), :]

    s, v = scores(i)
    s = jnp.where(col <= row, s, -jnp.inf)
    m = jnp.max(s, axis=1, keepdims=True)
    p = jnp.exp(s - m)
    l = jnp.sum(p, axis=1, keepdims=True)
    acc = _dot(p.astype(BF16), v)

    def body(n, state):
        m, l, acc = state
        s, v = scores(i - 1 - n)
        m_new = jnp.maximum(m, jnp.max(s, axis=1, keepdims=True))
        alpha = jnp.exp(m - m_new)
        p = jnp.exp(s - m_new)
        l = alpha * l + jnp.sum(p, axis=1, keepdims=True)
        acc = alpha * acc + _dot(p.astype(BF16), v)
        return m_new, l, acc

    m, l, acc = lax.fori_loop(0, i, body, (m, l, acc))
    o_ref[...] = (acc / l).astype(o_ref.dtype)


def _prompt_mixers(qq, kvb, cum_k, seq, heads):
    blk = _pick(seq, 256, LANES)
    grid = (heads, seq // blk)
    q_blk = (blk, HEAD_DIM)
    kv_blk = (seq, HEAD_DIM)
    out_spec = pl.BlockSpec(q_blk, lambda h, i: (i, h))
    out_shape = jax.ShapeDtypeStruct((seq, heads * HEAD_DIM), BF16)

    o_a = pl.pallas_call(
        functools.partial(_sb_prompt_kernel, blk=blk),
        grid=grid,
        in_specs=[pl.BlockSpec(q_blk, lambda h, i: (i, h)),
                  pl.BlockSpec(kv_blk, lambda h, i: (0, h)),
                  pl.BlockSpec(kv_blk, lambda h, i: (0, heads + h))],
        out_specs=out_spec,
        out_shape=out_shape,
        compiler_params=_params("parallel", "arbitrary"),
        name="sb_prompt",
    )(qq, kvb, kvb)

    o_b = pl.pallas_call(
        functools.partial(_fox_prompt_kernel, blk=blk),
        grid=grid,
        in_specs=[pl.BlockSpec(q_blk, lambda h, i: (i, heads + h)),
                  pl.BlockSpec(kv_blk, lambda h, i: (0, 2 * heads + h)),
                  pl.BlockSpec(kv_blk, lambda h, i: (0, 3 * heads + h)),
                  pl.BlockSpec((None, 1, seq), lambda h, i: (h, 0, 0))],
        out_specs=out_spec,
        out_shape=out_shape,
        compiler_params=_params("parallel", "arbitrary"),
        name="fox_prompt",
    )(qq, kvb, kvb, cum_k)
    return o_a, o_b


def _sample_kernel(qa_ref, qb_ref, kna_ref, vna_ref, knb_ref, vnb_ref,
                   kca_ref, vca_ref, kcb_ref, vcb_ref, cum_ref, oa_ref, ob_ref, *, past, tn):
    pad = jnp.zeros((LANES - tn, HEAD_DIM), BF16)
    row = lax.broadcasted_iota(jnp.int32, (tn, LANES), 0)
    col = lax.broadcasted_iota(jnp.int32, (tn, LANES), 1)
    suffix = _suffix_ones(LANES)
    n_chunks = past // LANES

    q = qa_ref[...]
    k_new = jnp.concatenate([kna_ref[...], pad], axis=0)
    v_new = jnp.concatenate([vna_ref[...], pad], axis=0)
    acc, carry = _sb_block(q, k_new, v_new, suffix, jnp.zeros((tn, 1), F32), col < row)
    for c in reversed(range(n_chunks)):
        k = kca_ref[c * LANES:(c + 1) * LANES, :].astype(BF16)
        v = vca_ref[c * LANES:(c + 1) * LANES, :].astype(BF16)
        out, carry = _sb_block(q, k, v, suffix, carry, None)
        acc = acc + out
    oa_ref[...] = acc.astype(oa_ref.dtype)

    q = qb_ref[...]
    k_new = jnp.concatenate([knb_ref[...], pad], axis=0)
    v_new = jnp.concatenate([vnb_ref[...], pad], axis=0)
    s_new = _dot_nt(q, k_new) - cum_ref[:, past:past + LANES]
    s_new = jnp.where(col <= row, s_new, -jnp.inf)
    s_old = _dot_nt(q, kcb_ref[...].astype(BF16)) - cum_ref[:, 0:past]
    m = jnp.maximum(jnp.max(s_new, axis=1, keepdims=True), jnp.max(s_old, axis=1, keepdims=True))
    p_new = jnp.exp(s_new - m)
    p_old = jnp.exp(s_old - m)
    l = jnp.sum(p_new, axis=1, keepdims=True) + jnp.sum(p_old, axis=1, keepdims=True)
    acc = _dot(p_new.astype(BF16), v_new) + _dot(p_old.astype(BF16), vcb_ref[...].astype(BF16))
    ob_ref[...] = (acc / l).astype(ob_ref.dtype)


def _sample_mixers(qq, kvb, caches, layer, cum_s, seq, heads, n_streams, tn):
    past = caches[0].shape[2]
    row0 = seq // tn
    new_blk = (tn, HEAD_DIM)

    def new_spec(col0):
        return pl.BlockSpec(new_blk, lambda b, h: (row0 + b, col0 + h))

    cache_spec = pl.BlockSpec((None, None, past, HEAD_DIM), lambda b, h: (layer, b, 0, h))
    out_spec = pl.BlockSpec(new_blk, lambda b, h: (b, h))
    out_shape = jax.ShapeDtypeStruct((n_streams * tn, heads * HEAD_DIM), BF16)
    return pl.pallas_call(
        functools.partial(_sample_kernel, past=past, tn=tn),
        grid=(n_streams, heads),
        in_specs=[new_spec(0), new_spec(heads),
                  new_spec(0), new_spec(heads), new_spec(2 * heads), new_spec(3 * heads),
                  cache_spec, cache_spec, cache_spec, cache_spec,
                  pl.BlockSpec((None, None, 1, past + LANES), lambda b, h: (b, h, 0, 0))],
        out_specs=[out_spec, out_spec],
        out_shape=[out_shape, out_shape],
        compiler_params=_params("parallel", "arbitrary"),
        name="sample_mixers",
    )(qq, qq, kvb, kvb, kvb, kvb, *caches, cum_s)


def _mix_kernel(oa_ref, ob_ref, xn_ref, wpa_ref, wpb_ref, wga_ref, wgb_ref, ba_ref, bb_ref, o_ref):
    xn = xn_ref[...]
    g_a = jax.nn.sigmoid(_dot(xn, wga_ref[...]) + ba_ref[...])
    g_b = jax.nn.sigmoid(_dot(xn, wgb_ref[...]) + bb_ref[...])
    mix = g_a * _dot(oa_ref[...], wpa_ref[...]) + g_b * _dot(ob_ref[...], wpb_ref[...])
    o_ref[...] = mix.astype(o_ref.dtype)


def _gated_mix(o_a, o_b, xn, w_proj_a, w_proj_b, w_gate, b_gate):
    rows, d = xn.shape
    width = o_a.shape[1]
    tm = _pick(rows, 832, BF16_ROWS)
    tn = _pick(d, 256, LANES)
    nb = d // tn
    b2 = b_gate.reshape(1, 2 * d)
    return pl.pallas_call(
        _mix_kernel,
        grid=(rows // tm, nb),
        in_specs=[pl.BlockSpec((tm, width), lambda i, j: (i, 0)),
                  pl.BlockSpec((tm, width), lambda i, j: (i, 0)),
                  pl.BlockSpec((tm, d), lambda i, j: (i, 0)),
                  pl.BlockSpec((width, tn), lambda i, j: (0, j)),
                  pl.BlockSpec((width, tn), lambda i, j: (0, j)),
                  pl.BlockSpec((d, tn), lambda i, j: (0, j)),
                  pl.BlockSpec((d, tn), lambda i, j: (0, nb + j)),
                  pl.BlockSpec((1, tn), lambda i, j: (0, j)),
                  pl.BlockSpec((1, tn), lambda i, j: (0, nb + j))],
        out_specs=pl.BlockSpec((tm, tn), lambda i, j: (i, j)),
        out_shape=jax.ShapeDtypeStruct((rows, d), BF16),
        compiler_params=_params("parallel", "arbitrary"),
        name="gated_mix",
    )(o_a, o_b, xn, w_proj_a, w_proj_b, w_gate, w_gate, b2, b2)


def _residual_matmul_kernel(a_ref, w_ref, r_ref, o_ref):
    o_ref[...] = r_ref[...] + _dot(a_ref[...], w_ref[...])


def _residual_matmul(a, w, resid, tm_target, tn_target):
    rows, k = a.shape
    n = w.shape[1]
    tm = _pick(rows, tm_target, BF16_ROWS)
    tn = _pick(n, tn_target, LANES)
    return pl.pallas_call(
        _residual_matmul_kernel,
        grid=(rows // tm, n // tn),
        in_specs=[pl.BlockSpec((tm, k), lambda i, j: (i, 0)),
                  pl.BlockSpec((k, tn), lambda i, j: (0, j)),
                  pl.BlockSpec((tm, tn), lambda i, j: (i, j))],
        out_specs=pl.BlockSpec((tm, tn), lambda i, j: (i, j)),
        out_shape=jax.ShapeDtypeStruct((rows, n), F32),
        compiler_params=_params("parallel", "arbitrary"),
        name="residual_matmul",
    )(a, w, resid)


def _swiglu_up_kernel(x_ref, wg_ref, wu_ref, o_ref):
    x = x_ref[...]
    g = _dot(x, wg_ref[...])
    o_ref[...] = (g * jax.nn.sigmoid(g) * _dot(x, wu_ref[...])).astype(o_ref.dtype)


def _swiglu_up(hn, w_gate, w_up):
    rows, d = hn.shape
    dff = w_gate.shape[1]
    tm = _pick(rows, 1040, BF16_ROWS)
    tn = _pick(dff, 256, LANES)
    return pl.pallas_call(
        _swiglu_up_kernel,
        grid=(rows // tm, dff // tn),
        in_specs=[pl.BlockSpec((tm, d), lambda i, j: (i, 0)),
                  pl.BlockSpec((d, tn), lambda i, j: (0, j)),
                  pl.BlockSpec((d, tn), lambda i, j: (0, j))],
        out_specs=pl.BlockSpec((tm, tn), lambda i, j: (i, j)),
        out_shape=jax.ShapeDtypeStruct((rows, dff), BF16),
        compiler_params=_params("parallel", "arbitrary"),
        name="swiglu_up",
    )(hn, w_gate, w_up)


def kernel(x_prompt, x_sample, cache_sb_k, cache_sb_v, cache_fox_k, cache_fox_v, cache_fox_logf, norm_attn, w_in, b_forget, q_norm, k_norm, w_proj_a, w_proj_b, w_gate_br, b_gate_br, w_out, norm_ffn, w_ffn_gate, w_ffn_up, w_ffn_down):
    batch, seq, d = x_prompt.shape
    n_streams, tn, _ = x_sample.shape
    depth, _, past, heads, _ = cache_sb_k.shape
    assert batch == 1 and seq % tn == 0 and tn % BF16_ROWS == 0 and past % LANES == 0
    width = heads * HEAD_DIM
    n_new = n_streams * tn

    x = jnp.concatenate([x_prompt.reshape(seq, d), x_sample.reshape(n_new, d)], axis=0)
    caches = [c.reshape(depth, n_streams, past, width)
              for c in (cache_sb_k, cache_sb_v, cache_fox_k, cache_fox_v)]
    logf_past = jnp.transpose(cache_fox_logf, (0, 1, 3, 2)).astype(F32)

    p_out = [[] for _ in range(5)]
    s_out = [[] for _ in range(5)]
    for l in range(depth):
        w_main = w_in[l, :, :6 * width].astype(BF16)
        wf_t = jnp.transpose(w_in[l, :, 6 * width:]).astype(BF16)

        xn = _rmsnorm(x, norm_attn[l])
        qq, kvf, kvb = _project_qkv(xn, w_main, q_norm[l], k_norm[l], width)
        logf_t = _log_forget(xn, wf_t, b_forget[l])

        cum_p = _cumsum_lanes(logf_t[:, :seq]).reshape(heads, 1, seq)
        logf_new = jnp.transpose(logf_t[:, seq:].reshape(heads, n_streams, tn), (1, 0, 2))
        logf_all = jnp.concatenate(
            [logf_past[l], logf_new, jnp.zeros((n_streams, heads, LANES - tn), F32)], axis=2)
        cum_s = _cumsum_lanes(logf_all.reshape(n_streams * heads, past + LANES))
        cum_s = cum_s.reshape(n_streams, heads, 1, past + LANES)

        oa_p, ob_p = _prompt_mixers(qq, kvb, cum_p, seq, heads)
        oa_s, ob_s = _sample_mixers(qq, kvb, caches, l, cum_s, seq, heads, n_streams, tn)
        o_a = jnp.concatenate([oa_p, oa_s], axis=0)
        o_b = jnp.concatenate([ob_p, ob_s], axis=0)

        mix = _gated_mix(o_a, o_b, xn, w_proj_a[l].astype(BF16), w_proj_b[l].astype(BF16),
                         w_gate_br[l].astype(BF16), b_gate_br[l])
        h = _residual_matmul(mix, w_out[l].astype(BF16), x, 1040, 512)
        hn = _rmsnorm(h, norm_ffn[l])
        act = _swiglu_up(hn, w_ffn_gate[l].astype(BF16), w_ffn_up[l].astype(BF16))
        x = _residual_matmul(act, w_ffn_down[l].astype(BF16), h, 640, 256)

        logf_rows = jnp.transpose(logf_t)
        for n in range(4):
            sect = kvf[:, n * width:(n + 1) * width]
            p_out[n].append(sect[:seq].reshape(batch, seq, heads, HEAD_DIM))
            s_out[n].append(sect[seq:].reshape(n_streams, tn, heads, HEAD_DIM))
        p_out[4].append(logf_rows[:seq].reshape(batch, seq, heads))
        s_out[4].append(logf_rows[seq:].reshape(n_streams, tn, heads))

    return (x[:seq].reshape(batch, seq, d), x[seq:].reshape(n_streams, tn, d),
            *[jnp.stack(a) for a in p_out], *[jnp.stack(a) for a in s_out])
```

```python
import functools

import jax
import jax.numpy as jnp
from jax import lax
from jax.experimental import pallas as pl
from jax.experimental.pallas import tpu as pltpu

EPS = 1e-6
HEAD_DIM = 128
LANES = 128
BF16_ROWS = 16
F32_ROWS = 8
CACHE_HEADS = 8
VMEM_LIMIT = 56 * 1024 * 1024
F32 = jnp.float32
BF16 = jnp.bfloat16


def _pick(n, target, mult):
    best = None
    for d in range(mult, min(n, target) + 1, mult):
        if n % d == 0:
            best = d
    return best if best is not None else n


def _params(*sem):
    return pltpu.CompilerParams(dimension_semantics=sem, vmem_limit_bytes=VMEM_LIMIT)


def _dot(a, b):
    return jnp.dot(a, b, preferred_element_type=F32)


def _dot_nt(a, b):
    return lax.dot_general(a, b, (((1,), (1,)), ((), ())), preferred_element_type=F32)


def _log_sigmoid(x):
    return jnp.minimum(x, 0.0) - jnp.log1p(jnp.exp(-jnp.abs(x)))


def _rmsnorm_kernel(x_ref, g_ref, o_ref):
    x = x_ref[...]
    y = x * lax.rsqrt(jnp.mean(x * x, axis=-1, keepdims=True) + EPS)
    o_ref[...] = (y * g_ref[...]).astype(o_ref.dtype)


def _rmsnorm(x, g):
    rows, d = x.shape
    tm = _pick(rows, 640, BF16_ROWS)
    return pl.pallas_call(
        _rmsnorm_kernel,
        grid=(rows // tm,),
        in_specs=[pl.BlockSpec((tm, d), lambda i: (i, 0)), pl.BlockSpec((1, d), lambda i: (0, 0))],
        out_specs=pl.BlockSpec((tm, d), lambda i: (i, 0)),
        out_shape=jax.ShapeDtypeStruct((rows, d), BF16),
        compiler_params=_params("parallel"),
        name="rmsnorm",
    )(x, g.reshape(1, d))


def _head_rmsnorm(acc, gain):
    outs = []
    for c in range(acc.shape[1] // HEAD_DIM):
        blk = acc[:, c * HEAD_DIM:(c + 1) * HEAD_DIM]
        y = blk * lax.rsqrt(jnp.mean(blk * blk, axis=-1, keepdims=True) + EPS)
        outs.append(y * gain)
    return outs[0] if len(outs) == 1 else jnp.concatenate(outs, axis=1)


def _proj_q_kernel(xn_ref, w_ref, gain_ref, o_ref, *, nb, scale):
    acc = _dot(xn_ref[...], w_ref[...])

    @pl.when(pl.program_id(1) < nb)
    def _():
        o_ref[...] = (acc * scale).astype(o_ref.dtype)

    @pl.when(pl.program_id(1) >= nb)
    def _():
        o_ref[...] = (_head_rmsnorm(acc, gain_ref[...]) * scale).astype(o_ref.dtype)


def _proj_kv_kernel(xn_ref, w_ref, gain_ref, of_ref, ob_ref, *, nb):
    acc = _dot(xn_ref[...], w_ref[...])
    j = pl.program_id(1)
    is_kfox = jnp.logical_and(j >= 2 * nb, j < 3 * nb)

    @pl.when(jnp.logical_not(is_kfox))
    def _():
        of_ref[...] = acc
        ob_ref[...] = acc.astype(ob_ref.dtype)

    @pl.when(is_kfox)
    def _():
        y = _head_rmsnorm(acc, gain_ref[...])
        of_ref[...] = y
        ob_ref[...] = y.astype(ob_ref.dtype)


def _project_qkv(xn, w_main, q_gain, k_gain, width):
    rows, d = xn.shape
    tm = _pick(rows, 1040, BF16_ROWS)
    tn = _pick(width, 512, LANES)
    nb = width // tn
    scale = HEAD_DIM ** -0.5
    x_spec = pl.BlockSpec((tm, d), lambda i, j: (i, 0))
    g_spec = pl.BlockSpec((1, HEAD_DIM), lambda i, j: (0, 0))
    o_spec = pl.BlockSpec((tm, tn), lambda i, j: (i, j))

    qq = pl.pallas_call(
        functools.partial(_proj_q_kernel, nb=nb, scale=scale),
        grid=(rows // tm, 2 * nb),
        in_specs=[x_spec,
                  pl.BlockSpec((d, tn), lambda i, j: (0, jnp.where(j < nb, j, j + 2 * nb))),
                  g_spec],
        out_specs=o_spec,
        out_shape=jax.ShapeDtypeStruct((rows, 2 * width), BF16),
        compiler_params=_params("parallel", "arbitrary"),
        name="proj_q",
    )(xn, w_main, q_gain.reshape(1, HEAD_DIM))

    kvf, kvb = pl.pallas_call(
        functools.partial(_proj_kv_kernel, nb=nb),
        grid=(rows // tm, 4 * nb),
        in_specs=[x_spec,
                  pl.BlockSpec((d, tn), lambda i, j: (0, jnp.where(j < 2 * nb, j + nb, j + 2 * nb))),
                  g_spec],
        out_specs=[o_spec, o_spec],
        out_shape=[jax.ShapeDtypeStruct((rows, 4 * width), F32),
                   jax.ShapeDtypeStruct((rows, 4 * width), BF16)],
        compiler_params=_params("parallel", "arbitrary"),
        name="proj_kv",
    )(xn, w_main, k_gain.reshape(1, HEAD_DIM))
    return qq, kvf, kvb


def _logf_kernel(wf_ref, xn_ref, b_ref, o_ref):
    z = _dot_nt(wf_ref[...], xn_ref[...]) + b_ref[...]
    o_ref[...] = _log_sigmoid(z)


def _log_forget(xn, wf_t, b_forget):
    rows, d = xn.shape
    heads = wf_t.shape[0]
    tt = _pick(rows, 1664, LANES)
    return pl.pallas_call(
        _logf_kernel,
        grid=(rows // tt,),
        in_specs=[pl.BlockSpec((heads, d), lambda i: (0, 0)),
                  pl.BlockSpec((tt, d), lambda i: (i, 0)),
                  pl.BlockSpec((heads, 1), lambda i: (0, 0))],
        out_specs=pl.BlockSpec((heads, tt), lambda i: (0, i)),
        out_shape=jax.ShapeDtypeStruct((heads, rows), F32),
        compiler_params=_params("parallel"),
        name="log_forget",
    )(wf_t, xn, b_forget.reshape(heads, 1))


def _split3(x):
    hi = x.astype(BF16)
    r1 = x - hi.astype(F32)
    mid = r1.astype(BF16)
    lo = (r1 - mid.astype(F32)).astype(BF16)
    return hi, mid, lo


def _cumsum_kernel(x_ref, o_ref):
    rows, length = x_ref.shape
    r = lax.broadcasted_iota(jnp.int32, (LANES, LANES), 0)
    c = lax.broadcasted_iota(jnp.int32, (LANES, LANES), 1)
    tri = jnp.where(r <= c, 1.0, 0.0).astype(BF16)

    def body(n, carry):
        off = pl.multiple_of(n * LANES, LANES)
        hi, mid, lo = _split3(x_ref[:, pl.ds(off, LANES)])
        y = (_dot(hi, tri) + _dot(mid, tri)) + _dot(lo, tri) + carry
        o_ref[:, pl.ds(off, LANES)] = y
        return y[:, LANES - 1:LANES]

    lax.fori_loop(0, length // LANES, body, jnp.zeros((rows, 1), F32))


def _cumsum_lanes(x):
    return pl.pallas_call(
        _cumsum_kernel,
        out_shape=jax.ShapeDtypeStruct(x.shape, F32),
        compiler_params=pltpu.CompilerParams(vmem_limit_bytes=VMEM_LIMIT),
        name="cumsum",
    )(x)


QUERY_BLOCK = 512
SUFFIX_BLOCK = 256


def _tri_masks(rows, cols):
    row = lax.broadcasted_iota(jnp.int32, (rows, cols), 0)
    col = lax.broadcasted_iota(jnp.int32, (rows, cols), 1)
    return row, col


def _neg_suffix(blk):
    r, c = _tri_masks(2 * blk, blk)
    r = jnp.where(r >= blk, r - blk, r)
    return jnp.where(r >= c, -1.0, 0.0).astype(BF16)


def _sb_block(q, k, v, neg_suffix, state, valid):
    acc, carry = state
    sub = neg_suffix.shape[1]
    z = _dot_nt(q, k)
    softplus = jnp.maximum(z, 0.0) + jnp.log(1.0 + jnp.exp(-jnp.abs(z)))
    if valid is not None:
        softplus = jnp.where(valid, softplus, 0.0)
    hi = softplus.astype(BF16)
    lo = (softplus - hi.astype(F32)).astype(BF16)
    cols = [slice(c * sub, (c + 1) * sub) for c in range(z.shape[1] // sub)]
    tails = [_dot(jnp.concatenate([hi[:, c], lo[:, c]], axis=1), neg_suffix) for c in cols]
    for c, tail in reversed(list(zip(cols, tails))):
        tail = tail + carry
        a = jnp.exp(z[:, c] + tail)
        if valid is not None:
            a = jnp.where(valid[:, c], a, 0.0)
        acc = acc + _dot(a.astype(BF16), v[c, :])
        carry = tail[:, 0:1]
    return acc, carry


def _sb_prompt_kernel(q_ref, k_ref, v_ref, o_ref, *, bq, sub):
    i = pl.program_id(1)
    neg_suffix = _neg_suffix(sub)
    q = q_ref[...]
    row, col = _tri_masks(bq, bq)

    def step(j, state, valid):
        rows = pl.ds(pl.multiple_of(j * bq, bq), bq)
        return _sb_block(q, k_ref[rows, :], v_ref[rows, :], neg_suffix, state, valid)

    state = (jnp.zeros((bq, HEAD_DIM), F32), jnp.zeros((bq, 1), F32))
    state = step(i, state, col < row)
    acc, _ = lax.fori_loop(0, i, lambda n, st: step(i - 1 - n, st, None), state)
    o_ref[...] = acc.astype(o_ref.dtype)


def _fox_prompt_kernel(q_ref, k_ref, v_ref, ck_ref, o_ref, *, bq, sub):
    i = pl.program_id(1)
    ones = jnp.ones((sub, HEAD_DIM), BF16)
    q = q_ref[...]
    row, col = _tri_masks(bq, bq)

    def step(j, state, valid):
        m, acc = state
        rows = pl.ds(pl.multiple_of(j * bq, bq), bq)
        s = _dot_nt(q, k_ref[rows, :]) - ck_ref[:, rows]
        if valid is not None:
            s = jnp.where(valid, s, -jnp.inf)
        m_new = jnp.maximum(m, jnp.max(s, axis=1, keepdims=True))
        p = jnp.exp(s - m_new).astype(BF16)
        acc = jnp.exp(m - m_new) * acc
        for c in range(bq // sub):
            off = pl.multiple_of(j * bq + c * sub, sub)
            v1 = jnp.concatenate([v_ref[pl.ds(off, sub), :], ones], axis=1)
            acc = acc + _dot(p[:, c * sub:(c + 1) * sub], v1)
        return m_new, acc

    state = (jnp.full((bq, 1), -jnp.inf, F32), jnp.zeros((bq, 2 * HEAD_DIM), F32))
    state = step(i, state, col <= row)
    _, acc = lax.fori_loop(0, i, lambda n, st: step(i - 1 - n, st, None), state)
    o_ref[...] = (acc[:, :HEAD_DIM] / acc[:, HEAD_DIM:]).astype(o_ref.dtype)


def _prompt_mixers(qq, kvb, cum_k, seq, heads):
    sub = _pick(seq, SUFFIX_BLOCK, LANES)
    bq = _pick(seq, QUERY_BLOCK, sub)
    grid = (heads, seq // bq)
    q_blk = (bq, HEAD_DIM)
    kv_blk = (seq, HEAD_DIM)
    out_spec = pl.BlockSpec(q_blk, lambda h, i: (i, h))
    out_shape = jax.ShapeDtypeStruct((seq, heads * HEAD_DIM), BF16)

    o_a = pl.pallas_call(
        functools.partial(_sb_prompt_kernel, bq=bq, sub=sub),
        grid=grid,
        in_specs=[pl.BlockSpec(q_blk, lambda h, i: (i, h)),
                  pl.BlockSpec(kv_blk, lambda h, i: (0, h)),
                  pl.BlockSpec(kv_blk, lambda h, i: (0, heads + h))],
        out_specs=out_spec,
        out_shape=out_shape,
        compiler_params=_params("parallel", "arbitrary"),
        name="sb_prompt",
    )(qq, kvb, kvb)

    o_b = pl.pallas_call(
        functools.partial(_fox_prompt_kernel, bq=bq, sub=sub),
        grid=grid,
        in_specs=[pl.BlockSpec(q_blk, lambda h, i: (i, heads + h)),
                  pl.BlockSpec(kv_blk, lambda h, i: (0, 2 * heads + h)),
                  pl.BlockSpec(kv_blk, lambda h, i: (0, 3 * heads + h)),
                  pl.BlockSpec((None, 1, seq), lambda h, i: (h, 0, 0))],
        out_specs=out_spec,
        out_shape=out_shape,
        compiler_params=_params("parallel", "arbitrary"),
        name="fox_prompt",
    )(qq, kvb, kvb, cum_k)
    return o_a, o_b


def _sample_kernel(qa_ref, qb_ref, kna_ref, vna_ref, knb_ref, vnb_ref,
                   kca_ref, vca_ref, kcb_ref, vcb_ref, cum_ref, oa_ref, ob_ref, *, past, tn):
    pad = jnp.zeros((LANES - tn, HEAD_DIM), BF16)
    row, col = _tri_masks(tn, LANES)
    neg_suffix = _neg_suffix(LANES)
    n_chunks = past // LANES

    q = qa_ref[...]
    k_new = jnp.concatenate([kna_ref[...], pad], axis=0)
    v_new = jnp.concatenate([vna_ref[...], pad], axis=0)
    state = (jnp.zeros((tn, HEAD_DIM), F32), jnp.zeros((tn, 1), F32))
    state = _sb_block(q, k_new, v_new, neg_suffix, state, col < row)
    head = pl.program_id(1) % kca_ref.shape[1]
    for c in reversed(range(n_chunks)):
        k = kca_ref[c * LANES:(c + 1) * LANES, head, :].astype(BF16)
        v = vca_ref[c * LANES:(c + 1) * LANES, head, :].astype(BF16)
        state = _sb_block(q, k, v, neg_suffix, state, None)
    oa_ref[...] = state[0].astype(oa_ref.dtype)

    q = qb_ref[...]
    k_new = jnp.concatenate([knb_ref[...], pad], axis=0)
    v_new = jnp.concatenate([vnb_ref[...], pad], axis=0)
    s_new = _dot_nt(q, k_new) - cum_ref[:, past:past + LANES]
    s_new = jnp.where(col <= row, s_new, -jnp.inf)
    s_old = _dot_nt(q, kcb_ref[:, head, :].astype(BF16)) - cum_ref[:, 0:past]
    m = jnp.maximum(jnp.max(s_new, axis=1, keepdims=True), jnp.max(s_old, axis=1, keepdims=True))
    p_new = jnp.exp(s_new - m)
    p_old = jnp.exp(s_old - m)
    l = jnp.sum(p_new, axis=1, keepdims=True) + jnp.sum(p_old, axis=1, keepdims=True)
    acc = _dot(p_new.astype(BF16), v_new) + _dot(p_old.astype(BF16), vcb_ref[:, head, :].astype(BF16))
    ob_ref[...] = (acc / l).astype(ob_ref.dtype)


def _sample_mixers(qq, kvb, caches, layer, cum_s, seq, heads, n_streams, tn):
    past = caches[0].shape[2]
    row0 = seq // tn
    new_blk = (tn, HEAD_DIM)
    resident = _pick(heads, CACHE_HEADS, F32_ROWS)

    def new_spec(col0):
        return pl.BlockSpec(new_blk, lambda b, h: (row0 + b, col0 + h))

    cache_spec = pl.BlockSpec((None, None, past, resident, HEAD_DIM), lambda b, h: (layer, b, 0, h // resident, 0))
    out_spec = pl.BlockSpec(new_blk, lambda b, h: (b, h))
    out_shape = jax.ShapeDtypeStruct((n_streams * tn, heads * HEAD_DIM), BF16)
    return pl.pallas_call(
        functools.partial(_sample_kernel, past=past, tn=tn),
        grid=(n_streams, heads),
        in_specs=[new_spec(0), new_spec(heads),
                  new_spec(0), new_spec(heads), new_spec(2 * heads), new_spec(3 * heads),
                  cache_spec, cache_spec, cache_spec, cache_spec,
                  pl.BlockSpec((None, None, 1, past + LANES), lambda b, h: (b, h, 0, 0))],
        out_specs=[out_spec, out_spec],
        out_shape=[out_shape, out_shape],
        compiler_params=_params("parallel", "arbitrary"),
        name="sample_mixers",
    )(qq, qq, kvb, kvb, kvb, kvb, *caches, cum_s)


def _mix_kernel(oa_ref, ob_ref, xn_ref, wpa_ref, wpb_ref, wga_ref, wgb_ref, ba_ref, bb_ref, o_ref):
    xn = xn_ref[...]
    g_a = jax.nn.sigmoid(_dot(xn, wga_ref[...]) + ba_ref[...])
    g_b = jax.nn.sigmoid(_dot(xn, wgb_ref[...]) + bb_ref[...])
    mix = g_a * _dot(oa_ref[...], wpa_ref[...]) + g_b * _dot(ob_ref[...], wpb_ref[...])
    o_ref[...] = mix.astype(o_ref.dtype)


def _gated_mix(o_a, o_b, xn, w_proj_a, w_proj_b, w_gate, b_gate):
    rows, d = xn.shape
    width = o_a.shape[1]
    tm = _pick(rows, 832, BF16_ROWS)
    tn = _pick(d, 256, LANES)
    nb = d // tn
    b2 = b_gate.reshape(1, 2 * d)
    return pl.pallas_call(
        _mix_kernel,
        grid=(rows // tm, nb),
        in_specs=[pl.BlockSpec((tm, width), lambda i, j: (i, 0)),
                  pl.BlockSpec((tm, width), lambda i, j: (i, 0)),
                  pl.BlockSpec((tm, d), lambda i, j: (i, 0)),
                  pl.BlockSpec((width, tn), lambda i, j: (0, j)),
                  pl.BlockSpec((width, tn), lambda i, j: (0, j)),
                  pl.BlockSpec((d, tn), lambda i, j: (0, j)),
                  pl.BlockSpec((d, tn), lambda i, j: (0, nb + j)),
                  pl.BlockSpec((1, tn), lambda i, j: (0, j)),
                  pl.BlockSpec((1, tn), lambda i, j: (0, nb + j))],
        out_specs=pl.BlockSpec((tm, tn), lambda i, j: (i, j)),
        out_shape=jax.ShapeDtypeStruct((rows, d), BF16),
        compiler_params=_params("parallel", "arbitrary"),
        name="gated_mix",
    )(o_a, o_b, xn, w_proj_a, w_proj_b, w_gate, w_gate, b2, b2)


def _residual_matmul_kernel(a_ref, w_ref, r_ref, o_ref):
    o_ref[...] = r_ref[...] + _dot(a_ref[...], w_ref[...])


def _residual_matmul(a, w, resid, tm_target, tn_target):
    rows, k = a.shape
    n = w.shape[1]
    tm = _pick(rows, tm_target, BF16_ROWS)
    tn = _pick(n, tn_target, LANES)
    return pl.pallas_call(
        _residual_matmul_kernel,
        grid=(rows // tm, n // tn),
        in_specs=[pl.BlockSpec((tm, k), lambda i, j: (i, 0)),
                  pl.BlockSpec((k, tn), lambda i, j: (0, j)),
                  pl.BlockSpec((tm, tn), lambda i, j: (i, j))],
        out_specs=pl.BlockSpec((tm, tn), lambda i, j: (i, j)),
        out_shape=jax.ShapeDtypeStruct((rows, n), F32),
        compiler_params=_params("parallel", "arbitrary"),
        name="residual_matmul",
    )(a, w, resid)


def _swiglu_up_kernel(x_ref, wg_ref, wu_ref, o_ref):
    x = x_ref[...]
    g = _dot(x, wg_ref[...])
    o_ref[...] = (g * jax.nn.sigmoid(g) * _dot(x, wu_ref[...])).astype(o_ref.dtype)


def _swiglu_up(hn, w_gate, w_up):
    rows, d = hn.shape
    dff = w_gate.shape[1]
    tm = _pick(rows, 1040, BF16_ROWS)
    tn = _pick(dff, 256, LANES)
    return pl.pallas_call(
        _swiglu_up_kernel,
        grid=(rows // tm, dff // tn),
        in_specs=[pl.BlockSpec((tm, d), lambda i, j: (i, 0)),
                  pl.BlockSpec((d, tn), lambda i, j: (0, j)),
                  pl.BlockSpec((d, tn), lambda i, j: (0, j))],
        out_specs=pl.BlockSpec((tm, tn), lambda i, j: (i, j)),
        out_shape=jax.ShapeDtypeStruct((rows, dff), BF16),
        compiler_params=_params("parallel", "arbitrary"),
        name="swiglu_up",
    )(hn, w_gate, w_up)


def kernel(x_prompt, x_sample, cache_sb_k, cache_sb_v, cache_fox_k, cache_fox_v, cache_fox_logf, norm_attn, w_in, b_forget, q_norm, k_norm, w_proj_a, w_proj_b, w_gate_br, b_gate_br, w_out, norm_ffn, w_ffn_gate, w_ffn_up, w_ffn_down):
    batch, seq, d = x_prompt.shape
    n_streams, tn, _ = x_sample.shape
    depth, _, past, heads, _ = cache_sb_k.shape
    assert batch == 1 and seq % tn == 0 and tn % BF16_ROWS == 0 and past % LANES == 0
    width = heads * HEAD_DIM
    n_new = n_streams * tn

    x = jnp.concatenate([x_prompt.reshape(seq, d), x_sample.reshape(n_new, d)], axis=0)
    caches = [cache_sb_k, cache_sb_v, cache_fox_k, cache_fox_v]
    logf_past = jnp.transpose(cache_fox_logf, (0, 1, 3, 2)).astype(F32)

    p_out = [[] for _ in range(5)]
    s_out = [[] for _ in range(5)]
    for l in range(depth):
        w_main = w_in[l, :, :6 * width].astype(BF16)
        wf_t = jnp.transpose(w_in[l, :, 6 * width:]).astype(BF16)

        xn = _rmsnorm(x, norm_attn[l])
        qq, kvf, kvb = _project_qkv(xn, w_main, q_norm[l], k_norm[l], width)
        logf_t = _log_forget(xn, wf_t, b_forget[l])

        cum_p = _cumsum_lanes(logf_t[:, :seq]).reshape(heads, 1, seq)
        logf_new = jnp.transpose(logf_t[:, seq:].reshape(heads, n_streams, tn), (1, 0, 2))
        logf_all = jnp.concatenate(
            [logf_past[l], logf_new, jnp.zeros((n_streams, heads, LANES - tn), F32)], axis=2)
        cum_s = _cumsum_lanes(logf_all.reshape(n_streams * heads, past + LANES))
        cum_s = cum_s.reshape(n_streams, heads, 1, past + LANES)

        oa_p, ob_p = _prompt_mixers(qq, kvb, cum_p, seq, heads)
        oa_s, ob_s = _sample_mixers(qq, kvb, caches, l, cum_s, seq, heads, n_streams, tn)
        o_a = jnp.concatenate([oa_p, oa_s], axis=0)
        o_b = jnp.concatenate([ob_p, ob_s], axis=0)

        mix = _gated_mix(o_a, o_b, xn, w_proj_a[l].astype(BF16), w_proj_b[l].astype(BF16),
                         w_gate_br[l].astype(BF16), b_gate_br[l])
        h = _residual_matmul(mix, w_out[l].astype(BF16), x, 1040, 512)
        hn = _rmsnorm(h, norm_ffn[l])
        act = _swiglu_up(hn, w_ffn_gate[l].astype(BF16), w_ffn_up[l].astype(BF16))
        x = _residual_matmul(act, w_ffn_down[l].astype(BF16), h, 640, 256)

        logf_rows = jnp.transpose(logf_t)
        for n in range(4):
            sect = kvf[:, n * width:(n + 1) * width]
            p_out[n].append(sect[:seq].reshape(batch, seq, heads, HEAD_DIM))
            s_out[n].append(sect[seq:].reshape(n_streams, tn, heads, HEAD_DIM))
        p_out[4].append(logf_rows[:seq].reshape(batch, seq, heads))
        s_out[4].append(logf_rows[seq:].reshape(n_streams, tn, heads))

    return (x[:seq].reshape(batch, seq, d), x[seq:].reshape(n_streams, tn, d),
            *[jnp.stack(a) for a in p_out], *[jnp.stack(a) for a in s_out])
```

```python
import functools

import jax
import jax.numpy as jnp
from jax import lax
from jax.experimental import pallas as pl
from jax.experimental.pallas import tpu as pltpu

EPS = 1e-6
HEAD_DIM = 128
LANES = 128
BF16_ROWS = 16
F32_ROWS = 8
CACHE_HEADS = 8
VMEM_LIMIT = 56 * 1024 * 1024
F32 = jnp.float32
BF16 = jnp.bfloat16


def _pick(n, target, mult):
    best = None
    for d in range(mult, min(n, target) + 1, mult):
        if n % d == 0:
            best = d
    return best if best is not None else n


def _params(*sem):
    return pltpu.CompilerParams(dimension_semantics=sem, vmem_limit_bytes=VMEM_LIMIT)


def _dot(a, b):
    return jnp.dot(a, b, preferred_element_type=F32)


def _dot_nt(a, b):
    return lax.dot_general(a, b, (((1,), (1,)), ((), ())), preferred_element_type=F32)


def _log_sigmoid(x):
    return jnp.minimum(x, 0.0) - jnp.log1p(jnp.exp(-jnp.abs(x)))


def _rmsnorm_kernel(x_ref, g_ref, o_ref):
    x = x_ref[...]
    y = x * lax.rsqrt(jnp.mean(x * x, axis=-1, keepdims=True) + EPS)
    o_ref[...] = (y * g_ref[...]).astype(o_ref.dtype)


def _rmsnorm(x, g):
    rows, d = x.shape
    tm = _pick(rows, 640, BF16_ROWS)
    return pl.pallas_call(
        _rmsnorm_kernel,
        grid=(rows // tm,),
        in_specs=[pl.BlockSpec((tm, d), lambda i: (i, 0)), pl.BlockSpec((1, d), lambda i: (0, 0))],
        out_specs=pl.BlockSpec((tm, d), lambda i: (i, 0)),
        out_shape=jax.ShapeDtypeStruct((rows, d), BF16),
        compiler_params=_params("parallel"),
        name="rmsnorm",
    )(x, g.reshape(1, d))


def _head_rmsnorm(acc, gain):
    outs = []
    for c in range(acc.shape[1] // HEAD_DIM):
        blk = acc[:, c * HEAD_DIM:(c + 1) * HEAD_DIM]
        y = blk * lax.rsqrt(jnp.mean(blk * blk, axis=-1, keepdims=True) + EPS)
        outs.append(y * gain)
    return outs[0] if len(outs) == 1 else jnp.concatenate(outs, axis=1)


def _proj_q_kernel(xn_ref, w_ref, gain_ref, o_ref, *, nb, scale):
    acc = _dot(xn_ref[...], w_ref[...])

    @pl.when(pl.program_id(1) < nb)
    def _():
        o_ref[...] = (acc * scale).astype(o_ref.dtype)

    @pl.when(pl.program_id(1) >= nb)
    def _():
        o_ref[...] = (_head_rmsnorm(acc, gain_ref[...]) * scale).astype(o_ref.dtype)


def _project_q(xn, w_main, q_gain, width):
    rows, d = xn.shape
    tm = _pick(rows, 1040, BF16_ROWS)
    tn = _pick(width, 512, LANES)
    nb = width // tn
    return pl.pallas_call(
        functools.partial(_proj_q_kernel, nb=nb, scale=HEAD_DIM ** -0.5),
        grid=(rows // tm, 2 * nb),
        in_specs=[pl.BlockSpec((tm, d), lambda i, j: (i, 0)),
                  pl.BlockSpec((d, tn), lambda i, j: (0, jnp.where(j < nb, j, j + 2 * nb))),
                  pl.BlockSpec((1, HEAD_DIM), lambda i, j: (0, 0))],
        out_specs=pl.BlockSpec((tm, tn), lambda i, j: (i, j)),
        out_shape=jax.ShapeDtypeStruct((rows, 2 * width), BF16),
        compiler_params=_params("parallel", "arbitrary"),
        name="proj_q",
    )(xn, w_main, q_gain.reshape(1, HEAD_DIM))


def _proj_cache_kernel(xn_ref, w_ref, gain_ref, cache_in_ref, cache_ref, ob_ref, *, normed):
    del cache_in_ref
    y = _dot(xn_ref[...], w_ref[...])
    if normed:
        y = _head_rmsnorm(y, gain_ref[...])
    ob_ref[...] = y.astype(ob_ref.dtype)
    lead = cache_ref.shape[:-2]
    for h in range(cache_ref.shape[-2]):
        piece = y[:, h * HEAD_DIM:(h + 1) * HEAD_DIM].reshape(lead + (HEAD_DIM,))
        if len(lead) == 1:
            cache_ref[:, h, :] = piece
        else:
            cache_ref[:, :, h, :] = piece


def _project_cache(xn, w_main, gain, w_section, normed, cache, layer, sample):
    rows, d = xn.shape
    _, b, t, heads, _ = cache.shape
    hb = _pick(heads, CACHE_HEADS, F32_ROWS)
    tn = hb * HEAD_DIM
    nj = heads // hb
    n = b * t
    if sample:
        tm = n
        x_spec = pl.BlockSpec((n, d), lambda i, j: ((rows - n) // n, 0))
        c_spec = pl.BlockSpec((None, b, t, hb, HEAD_DIM), lambda i, j: (layer, 0, 0, j, 0))
    else:
        assert b == 1
        tm = _pick(n, 1024, BF16_ROWS)
        x_spec = pl.BlockSpec((tm, d), lambda i, j: (i, 0))
        c_spec = pl.BlockSpec((None, None, tm, hb, HEAD_DIM), lambda i, j: (layer, 0, i, j, 0))
    return pl.pallas_call(
        functools.partial(_proj_cache_kernel, normed=normed),
        grid=(n // tm, nj),
        in_specs=[x_spec,
                  pl.BlockSpec((d, tn), lambda i, j: (0, w_section * nj + j)),
                  pl.BlockSpec((1, HEAD_DIM), lambda i, j: (0, 0)),
                  pl.BlockSpec(memory_space=pl.ANY)],
        out_specs=[c_spec, pl.BlockSpec((tm, tn), lambda i, j: (i, j))],
        out_shape=[jax.ShapeDtypeStruct(cache.shape, F32),
                   jax.ShapeDtypeStruct((n, heads * HEAD_DIM), BF16)],
        input_output_aliases={3: 0},
        compiler_params=_params("parallel", "arbitrary"),
        name="proj_cache",
    )(xn, w_main, gain.reshape(1, HEAD_DIM), cache)


def _logf_kernel(wf_ref, xn_ref, b_ref, o_ref):
    z = _dot_nt(wf_ref[...], xn_ref[...]) + b_ref[...]
    o_ref[...] = _log_sigmoid(z)


def _log_forget(xn, wf_t, b_forget):
    rows, d = xn.shape
    heads = wf_t.shape[0]
    tt = _pick(rows, 1664, LANES)
    return pl.pallas_call(
        _logf_kernel,
        grid=(rows // tt,),
        in_specs=[pl.BlockSpec((heads, d), lambda i: (0, 0)),
                  pl.BlockSpec((tt, d), lambda i: (i, 0)),
                  pl.BlockSpec((heads, 1), lambda i: (0, 0))],
        out_specs=pl.BlockSpec((heads, tt), lambda i: (0, i)),
        out_shape=jax.ShapeDtypeStruct((heads, rows), F32),
        compiler_params=_params("parallel"),
        name="log_forget",
    )(wf_t, xn, b_forget.reshape(heads, 1))


def _split3(x):
    hi = x.astype(BF16)
    r1 = x - hi.astype(F32)
    mid = r1.astype(BF16)
    lo = (r1 - mid.astype(F32)).astype(BF16)
    return hi, mid, lo


def _cumsum_kernel(x_ref, o_ref):
    rows, length = x_ref.shape
    r = lax.broadcasted_iota(jnp.int32, (LANES, LANES), 0)
    c = lax.broadcasted_iota(jnp.int32, (LANES, LANES), 1)
    tri = jnp.where(r <= c, 1.0, 0.0).astype(BF16)

    def body(n, carry):
        off = pl.multiple_of(n * LANES, LANES)
        hi, mid, lo = _split3(x_ref[:, pl.ds(off, LANES)])
        y = (_dot(hi, tri) + _dot(mid, tri)) + _dot(lo, tri) + carry
        o_ref[:, pl.ds(off, LANES)] = y
        return y[:, LANES - 1:LANES]

    lax.fori_loop(0, length // LANES, body, jnp.zeros((rows, 1), F32))


def _cumsum_lanes(x):
    return pl.pallas_call(
        _cumsum_kernel,
        out_shape=jax.ShapeDtypeStruct(x.shape, F32),
        compiler_params=pltpu.CompilerParams(vmem_limit_bytes=VMEM_LIMIT),
        name="cumsum",
    )(x)


ATTN_BLOCK = 1024
SUFFIX_BLOCK = 256
FOX_ROW_CHUNKS = 4


def _tri_masks(rows, cols):
    row = lax.broadcasted_iota(jnp.int32, (rows, cols), 0)
    col = lax.broadcasted_iota(jnp.int32, (rows, cols), 1)
    return row, col


def _neg_suffix(blk):
    r, c = _tri_masks(2 * blk, blk)
    r = jnp.where(r >= blk, r - blk, r)
    return jnp.where(r >= c, -1.0, 0.0).astype(BF16)


def _sb_block(q, k, v, neg_suffix, state, valid):
    acc, carry = state
    sub = neg_suffix.shape[1]
    z = _dot_nt(q, k)
    softplus = jnp.maximum(z, 0.0) + jnp.log(1.0 + jnp.exp(-jnp.abs(z)))
    if valid is not None:
        softplus = jnp.where(valid, softplus, 0.0)
    hi = softplus.astype(BF16)
    lo = (softplus - hi.astype(F32)).astype(BF16)
    cols = [slice(c * sub, (c + 1) * sub) for c in range(z.shape[1] // sub)]
    tails = [_dot(jnp.concatenate([hi[:, c], lo[:, c]], axis=1), neg_suffix) for c in cols]
    for c, tail in reversed(list(zip(cols, tails))):
        tail = tail + carry
        a = jnp.exp(z[:, c] + tail)
        if valid is not None:
            a = jnp.where(valid[:, c], a, 0.0)
        acc = acc + _dot(a.astype(BF16), v[c, :])
        carry = tail[:, 0:1]
    return acc, carry


def _causal_sweep(i, blk, step, state, strict):
    row, col = _tri_masks(blk, blk)
    state = step(i, state, col < row if strict else col <= row)
    return lax.fori_loop(0, i, lambda n, st: step(i - 1 - n, st, None), state)


def _sb_prompt_kernel(q_ref, k_ref, v_ref, o_in_ref, o_ref, *, blk, sub):
    del o_in_ref
    neg_suffix = _neg_suffix(sub)
    q = q_ref[...]

    def step(j, state, valid):
        rows = pl.ds(pl.multiple_of(j * blk, blk), blk)
        return _sb_block(q, k_ref[rows, :], v_ref[rows, :], neg_suffix, state, valid)

    state = (jnp.zeros((blk, HEAD_DIM), F32), jnp.zeros((blk, 1), F32))
    acc, _ = _causal_sweep(pl.program_id(1), blk, step, state, strict=True)
    o_ref[...] = acc.astype(o_ref.dtype)


def _fox_prompt_kernel(q_ref, k_ref, v_ref, ck_ref, o_in_ref, o_ref, *, blk, chunks):
    del o_in_ref
    ones = jnp.ones((blk, HEAD_DIM), BF16)
    tq = blk // chunks

    def step(j, state, valid):
        rows = pl.ds(pl.multiple_of(j * blk, blk), blk)
        k = k_ref[rows, :]
        ck = ck_ref[:, rows]
        v1 = jnp.concatenate([v_ref[rows, :], ones], axis=1)
        new = []
        for c, (m, acc) in enumerate(state):
            s = _dot_nt(q_ref[c * tq:(c + 1) * tq, :], k) - ck
            if valid is not None:
                s = jnp.where(valid[c * tq:(c + 1) * tq], s, -jnp.inf)
            m_new = jnp.maximum(m, jnp.max(s, axis=1, keepdims=True))
            p = jnp.exp(s - m_new).astype(BF16)
            new.append((m_new, jnp.exp(m - m_new) * acc + _dot(p, v1)))
        return tuple(new)

    state = tuple((jnp.full((tq, 1), -jnp.inf, F32), jnp.zeros((tq, 2 * HEAD_DIM), F32)) for _ in range(chunks))
    state = _causal_sweep(pl.program_id(1), blk, step, state, strict=False)
    for c, (_, acc) in enumerate(state):
        o_ref[c * tq:(c + 1) * tq, :] = (acc[:, :HEAD_DIM] / acc[:, HEAD_DIM:]).astype(o_ref.dtype)


def _prompt_mixers(qq, kv, cum_k, o_a, o_b, seq, heads):
    sub = _pick(seq, SUFFIX_BLOCK, LANES)
    blk = _pick(seq, ATTN_BLOCK, sub)
    chunks = FOX_ROW_CHUNKS if blk % (FOX_ROW_CHUNKS * BF16_ROWS) == 0 else 1
    grid = (heads, seq // blk)
    q_blk = (blk, HEAD_DIM)
    kv_spec = pl.BlockSpec((seq, HEAD_DIM), lambda h, i: (0, h))
    any_spec = pl.BlockSpec(memory_space=pl.ANY)
    out_spec = pl.BlockSpec(q_blk, lambda h, i: (i, h))
    out_shape = jax.ShapeDtypeStruct(o_a.shape, BF16)

    o_a = pl.pallas_call(
        functools.partial(_sb_prompt_kernel, blk=blk, sub=sub),
        grid=grid,
        in_specs=[pl.BlockSpec(q_blk, lambda h, i: (i, h)), kv_spec, kv_spec, any_spec],
        out_specs=out_spec,
        out_shape=out_shape,
        input_output_aliases={3: 0},
        compiler_params=_params("parallel", "arbitrary"),
        name="sb_prompt",
    )(qq, kv[0], kv[1], o_a)

    o_b = pl.pallas_call(
        functools.partial(_fox_prompt_kernel, blk=blk, chunks=chunks),
        grid=grid,
        in_specs=[pl.BlockSpec(q_blk, lambda h, i: (i, heads + h)), kv_spec, kv_spec,
                  pl.BlockSpec((None, 1, seq), lambda h, i: (h, 0, 0)), any_spec],
        out_specs=out_spec,
        out_shape=out_shape,
        input_output_aliases={4: 0},
        compiler_params=_params("parallel", "arbitrary"),
        name="fox_prompt",
    )(qq, kv[2], kv[3], cum_k, o_b)
    return o_a, o_b


def _sample_kernel(qa_ref, qb_ref, kna_ref, vna_ref, knb_ref, vnb_ref,
                   kca_ref, vca_ref, kcb_ref, vcb_ref, cum_ref, oa_in_ref, ob_in_ref, oa_ref, ob_ref, *, past, tn):
    del oa_in_ref, ob_in_ref
    pad = jnp.zeros((LANES - tn, HEAD_DIM), BF16)
    row, col = _tri_masks(tn, LANES)
    neg_suffix = _neg_suffix(LANES)
    n_chunks = past // LANES

    q = qa_ref[...]
    k_new = jnp.concatenate([kna_ref[...], pad], axis=0)
    v_new = jnp.concatenate([vna_ref[...], pad], axis=0)
    state = (jnp.zeros((tn, HEAD_DIM), F32), jnp.zeros((tn, 1), F32))
    state = _sb_block(q, k_new, v_new, neg_suffix, state, col < row)
    head = pl.program_id(1) % kca_ref.shape[1]
    for c in reversed(range(n_chunks)):
        k = kca_ref[c * LANES:(c + 1) * LANES, head, :].astype(BF16)
        v = vca_ref[c * LANES:(c + 1) * LANES, head, :].astype(BF16)
        state = _sb_block(q, k, v, neg_suffix, state, None)
    oa_ref[...] = state[0].astype(oa_ref.dtype)

    q = qb_ref[...]
    k_new = jnp.concatenate([knb_ref[...], pad], axis=0)
    v_new = jnp.concatenate([vnb_ref[...], pad], axis=0)
    s_new = _dot_nt(q, k_new) - cum_ref[:, past:past + LANES]
    s_new = jnp.where(col <= row, s_new, -jnp.inf)
    s_old = _dot_nt(q, kcb_ref[:, head, :].astype(BF16)) - cum_ref[:, 0:past]
    m = jnp.maximum(jnp.max(s_new, axis=1, keepdims=True), jnp.max(s_old, axis=1, keepdims=True))
    p_new = jnp.exp(s_new - m)
    p_old = jnp.exp(s_old - m)
    l = jnp.sum(p_new, axis=1, keepdims=True) + jnp.sum(p_old, axis=1, keepdims=True)
    acc = _dot(p_new.astype(BF16), v_new) + _dot(p_old.astype(BF16), vcb_ref[:, head, :].astype(BF16))
    ob_ref[...] = (acc / l).astype(ob_ref.dtype)


def _sample_mixers(qq, kv_new, caches, layer, cum_s, o_a, o_b, heads, n_streams, tn):
    past = caches[0].shape[2]
    row0 = o_a.shape[0] // tn - n_streams
    new_blk = (tn, HEAD_DIM)
    resident = _pick(heads, CACHE_HEADS, F32_ROWS)
    q_specs = [pl.BlockSpec(new_blk, lambda b, h: (row0 + b, h)),
               pl.BlockSpec(new_blk, lambda b, h: (row0 + b, heads + h))]
    new_spec = pl.BlockSpec(new_blk, lambda b, h: (b, h))
    cache_spec = pl.BlockSpec((None, None, past, resident, HEAD_DIM), lambda b, h: (layer, b, 0, h // resident, 0))
    any_spec = pl.BlockSpec(memory_space=pl.ANY)
    out_spec = pl.BlockSpec(new_blk, lambda b, h: (row0 + b, h))
    out_shape = jax.ShapeDtypeStruct(o_a.shape, BF16)
    return pl.pallas_call(
        functools.partial(_sample_kernel, past=past, tn=tn),
        grid=(n_streams, heads),
        in_specs=q_specs + [new_spec] * 4 + [cache_spec] * 4
                 + [pl.BlockSpec((None, None, 1, past + LANES), lambda b, h: (b, h, 0, 0)), any_spec, any_spec],
        out_specs=[out_spec, out_spec],
        out_shape=[out_shape, out_shape],
        input_output_aliases={11: 0, 12: 1},
        compiler_params=_params("parallel", "arbitrary"),
        name="sample_mixers",
    )(qq, qq, *kv_new, *caches, cum_s, o_a, o_b)


def _mix_kernel(oa_ref, ob_ref, xn_ref, wpa_ref, wpb_ref, wga_ref, wgb_ref, ba_ref, bb_ref, o_ref):
    xn = xn_ref[...]
    g_a = jax.nn.sigmoid(_dot(xn, wga_ref[...]) + ba_ref[...])
    g_b = jax.nn.sigmoid(_dot(xn, wgb_ref[...]) + bb_ref[...])
    mix = g_a * _dot(oa_ref[...], wpa_ref[...]) + g_b * _dot(ob_ref[...], wpb_ref[...])
    o_ref[...] = mix.astype(o_ref.dtype)


def _gated_mix(o_a, o_b, xn, w_proj_a, w_proj_b, w_gate, b_gate):
    rows, d = xn.shape
    width = o_a.shape[1]
    tm = _pick(rows, 832, BF16_ROWS)
    tn = _pick(d, 256, LANES)
    nb = d // tn
    b2 = b_gate.reshape(1, 2 * d)
    return pl.pallas_call(
        _mix_kernel,
        grid=(rows // tm, nb),
        in_specs=[pl.BlockSpec((tm, width), lambda i, j: (i, 0)),
                  pl.BlockSpec((tm, width), lambda i, j: (i, 0)),
                  pl.BlockSpec((tm, d), lambda i, j: (i, 0)),
                  pl.BlockSpec((width, tn), lambda i, j: (0, j)),
                  pl.BlockSpec((width, tn), lambda i, j: (0, j)),
                  pl.BlockSpec((d, tn), lambda i, j: (0, j)),
                  pl.BlockSpec((d, tn), lambda i, j: (0, nb + j)),
                  pl.BlockSpec((1, tn), lambda i, j: (0, j)),
                  pl.BlockSpec((1, tn), lambda i, j: (0, nb + j))],
        out_specs=pl.BlockSpec((tm, tn), lambda i, j: (i, j)),
        out_shape=jax.ShapeDtypeStruct((rows, d), BF16),
        compiler_params=_params("parallel", "arbitrary"),
        name="gated_mix",
    )(o_a, o_b, xn, w_proj_a, w_proj_b, w_gate, w_gate, b2, b2)


def _residual_matmul_kernel(a_ref, w_ref, r_ref, o_ref):
    o_ref[...] = r_ref[...] + _dot(a_ref[...], w_ref[...])


def _residual_matmul(a, w, resid, tm_target, tn_target):
    rows, k = a.shape
    n = w.shape[1]
    tm = _pick(rows, tm_target, BF16_ROWS)
    tn = _pick(n, tn_target, LANES)
    return pl.pallas_call(
        _residual_matmul_kernel,
        grid=(rows // tm, n // tn),
        in_specs=[pl.BlockSpec((tm, k), lambda i, j: (i, 0)),
                  pl.BlockSpec((k, tn), lambda i, j: (0, j)),
                  pl.BlockSpec((tm, tn), lambda i, j: (i, j))],
        out_specs=pl.BlockSpec((tm, tn), lambda i, j: (i, j)),
        out_shape=jax.ShapeDtypeStruct((rows, n), F32),
        compiler_params=_params("parallel", "arbitrary"),
        name="residual_matmul",
    )(a, w, resid)


def _swiglu_up_kernel(x_ref, wg_ref, wu_ref, o_ref):
    x = x_ref[...]
    g = _dot(x, wg_ref[...])
    o_ref[...] = (g * jax.nn.sigmoid(g) * _dot(x, wu_ref[...])).astype(o_ref.dtype)


def _swiglu_up(hn, w_gate, w_up):
    rows, d = hn.shape
    dff = w_gate.shape[1]
    tm = _pick(rows, 1040, BF16_ROWS)
    tn = _pick(dff, 256, LANES)
    return pl.pallas_call(
        _swiglu_up_kernel,
        grid=(rows // tm, dff // tn),
        in_specs=[pl.BlockSpec((tm, d), lambda i, j: (i, 0)),
                  pl.BlockSpec((d, tn), lambda i, j: (0, j)),
                  pl.BlockSpec((d, tn), lambda i, j: (0, j))],
        out_specs=pl.BlockSpec((tm, tn), lambda i, j: (i, j)),
        out_shape=jax.ShapeDtypeStruct((rows, dff), BF16),
        compiler_params=_params("parallel", "arbitrary"),
        name="swiglu_up",
    )(hn, w_gate, w_up)


def kernel(x_prompt, x_sample, cache_sb_k, cache_sb_v, cache_fox_k, cache_fox_v, cache_fox_logf, norm_attn, w_in, b_forget, q_norm, k_norm, w_proj_a, w_proj_b, w_gate_br, b_gate_br, w_out, norm_ffn, w_ffn_gate, w_ffn_up, w_ffn_down):
    batch, seq, d = x_prompt.shape
    n_streams, tn, _ = x_sample.shape
    depth, _, past, heads, _ = cache_sb_k.shape
    assert batch == 1 and seq % tn == 0 and tn % BF16_ROWS == 0 and past % LANES == 0
    width = heads * HEAD_DIM
    n_new = n_streams * tn
    rows = seq + n_new

    x = jnp.concatenate([x_prompt.reshape(seq, d), x_sample.reshape(n_new, d)], axis=0)
    caches = [cache_sb_k, cache_sb_v, cache_fox_k, cache_fox_v]
    logf_past = jnp.transpose(cache_fox_logf, (0, 1, 3, 2)).astype(F32)

    p_cache = [jnp.zeros((depth, batch, seq, heads, HEAD_DIM), F32) for _ in range(4)]
    s_cache = [jnp.zeros((depth, n_streams, tn, heads, HEAD_DIM), F32) for _ in range(4)]
    p_logf, s_logf = [], []
    kv_sections = ((1, False), (2, False), (4, True), (5, False))
    for l in range(depth):
        w_main = w_in[l, :, :6 * width].astype(BF16)
        wf_t = jnp.transpose(w_in[l, :, 6 * width:]).astype(BF16)

        xn = _rmsnorm(x, norm_attn[l])
        qq = _project_q(xn, w_main, q_norm[l], width)
        kv_p, kv_s = [], []
        for n, (section, normed) in enumerate(kv_sections):
            p_cache[n], kb = _project_cache(xn, w_main, k_norm[l], section, normed, p_cache[n], l, sample=False)
            s_cache[n], ks = _project_cache(xn, w_main, k_norm[l], section, normed, s_cache[n], l, sample=True)
            kv_p.append(kb)
            kv_s.append(ks)
        logf_t = _log_forget(xn, wf_t, b_forget[l])

        cum_p = _cumsum_lanes(logf_t[:, :seq]).reshape(heads, 1, seq)
        logf_new = jnp.transpose(logf_t[:, seq:].reshape(heads, n_streams, tn), (1, 0, 2))
        logf_all = jnp.concatenate(
            [logf_past[l], logf_new, jnp.zeros((n_streams, heads, LANES - tn), F32)], axis=2)
        cum_s = _cumsum_lanes(logf_all.reshape(n_streams * heads, past + LANES))
        cum_s = cum_s.reshape(n_streams, heads, 1, past + LANES)

        o_a = jnp.zeros((rows, width), BF16)
        o_b = jnp.zeros((rows, width), BF16)
        o_a, o_b = _sample_mixers(qq, kv_s, caches, l, cum_s, o_a, o_b, heads, n_streams, tn)
        o_a, o_b = _prompt_mixers(qq, kv_p, cum_p, o_a, o_b, seq, heads)

        mix = _gated_mix(o_a, o_b, xn, w_proj_a[l].astype(BF16), w_proj_b[l].astype(BF16),
                         w_gate_br[l].astype(BF16), b_gate_br[l])
        h = _residual_matmul(mix, w_out[l].astype(BF16), x, 1040, 512)
        hn = _rmsnorm(h, norm_ffn[l])
        act = _swiglu_up(hn, w_ffn_gate[l].astype(BF16), w_ffn_up[l].astype(BF16))
        x = _residual_matmul(act, w_ffn_down[l].astype(BF16), h, 640, 256)

        logf_rows = jnp.transpose(logf_t)
        p_logf.append(logf_rows[:seq].reshape(batch, seq, heads))
        s_logf.append(logf_rows[seq:].reshape(n_streams, tn, heads))

    return (x[:seq].reshape(batch, seq, d), x[seq:].reshape(n_streams, tn, d),
            *p_cache, jnp.stack(p_logf), *s_cache, jnp.stack(s_logf))
```

```python
import functools

import jax
import jax.numpy as jnp
from jax import lax
from jax.experimental import pallas as pl
from jax.experimental.pallas import tpu as pltpu

EPS = 1e-6
LOG2E = 1.4426950408889634
HEAD_DIM = 128
LANES = 128
BF16_ROWS = 16
F32_ROWS = 8
CACHE_HEADS = 8
VMEM_LIMIT = 56 * 1024 * 1024
F32 = jnp.float32
BF16 = jnp.bfloat16


def _pick(n, target, mult):
    best = None
    for d in range(mult, min(n, target) + 1, mult):
        if n % d == 0:
            best = d
    return best if best is not None else n


def _params(*sem):
    return pltpu.CompilerParams(dimension_semantics=sem, vmem_limit_bytes=VMEM_LIMIT)


def _dot(a, b):
    return jnp.dot(a, b, preferred_element_type=F32)


def _dot_nt(a, b):
    return lax.dot_general(a, b, (((1,), (1,)), ((), ())), preferred_element_type=F32)


def _log_sigmoid(x):
    return jnp.minimum(x, 0.0) - jnp.log1p(jnp.exp(-jnp.abs(x)))


def _rmsnorm_kernel(x_ref, g_ref, o_ref):
    x = x_ref[...]
    y = x * lax.rsqrt(jnp.mean(x * x, axis=-1, keepdims=True) + EPS)
    o_ref[...] = (y * g_ref[...]).astype(o_ref.dtype)


def _rmsnorm(x, g):
    rows, d = x.shape
    tm = _pick(rows, 640, BF16_ROWS)
    return pl.pallas_call(
        _rmsnorm_kernel,
        grid=(rows // tm,),
        in_specs=[pl.BlockSpec((tm, d), lambda i: (i, 0)), pl.BlockSpec((1, d), lambda i: (0, 0))],
        out_specs=pl.BlockSpec((tm, d), lambda i: (i, 0)),
        out_shape=jax.ShapeDtypeStruct((rows, d), BF16),
        compiler_params=_params("parallel"),
        name="rmsnorm",
    )(x, g.reshape(1, d))


def _head_rmsnorm(acc, gain):
    outs = []
    for c in range(acc.shape[1] // HEAD_DIM):
        blk = acc[:, c * HEAD_DIM:(c + 1) * HEAD_DIM]
        y = blk * lax.rsqrt(jnp.mean(blk * blk, axis=-1, keepdims=True) + EPS)
        outs.append(y * gain)
    return outs[0] if len(outs) == 1 else jnp.concatenate(outs, axis=1)


def _proj_q_kernel(xn_ref, w_ref, gain_ref, o_ref, *, nb, scale):
    acc = _dot(xn_ref[...], w_ref[...])

    @pl.when(pl.program_id(1) < nb)
    def _():
        o_ref[...] = (acc * scale).astype(o_ref.dtype)

    @pl.when(pl.program_id(1) >= nb)
    def _():
        o_ref[...] = (_head_rmsnorm(acc, gain_ref[...]) * scale).astype(o_ref.dtype)


def _project_q(xn, w_main, q_gain, width):
    rows, d = xn.shape
    tm = _pick(rows, 1040, BF16_ROWS)
    tn = _pick(width, 512, LANES)
    nb = width // tn
    return pl.pallas_call(
        functools.partial(_proj_q_kernel, nb=nb, scale=HEAD_DIM ** -0.5),
        grid=(rows // tm, 2 * nb),
        in_specs=[pl.BlockSpec((tm, d), lambda i, j: (i, 0)),
                  pl.BlockSpec((d, tn), lambda i, j: (0, jnp.where(j < nb, j, j + 2 * nb))),
                  pl.BlockSpec((1, HEAD_DIM), lambda i, j: (0, 0))],
        out_specs=pl.BlockSpec((tm, tn), lambda i, j: (i, j)),
        out_shape=jax.ShapeDtypeStruct((rows, 2 * width), BF16),
        compiler_params=_params("parallel", "arbitrary"),
        name="proj_q",
    )(xn, w_main, q_gain.reshape(1, HEAD_DIM))


def _proj_cache_kernel(xn_ref, w_ref, gain_ref, cache_in_ref, cache_ref, ob_ref, *, normed):
    del cache_in_ref
    y = _dot(xn_ref[...], w_ref[...])
    if normed:
        y = _head_rmsnorm(y, gain_ref[...])
    ob_ref[...] = y.astype(ob_ref.dtype)
    lead = cache_ref.shape[:-2]
    for h in range(cache_ref.shape[-2]):
        piece = y[:, h * HEAD_DIM:(h + 1) * HEAD_DIM].reshape(lead + (HEAD_DIM,))
        if len(lead) == 1:
            cache_ref[:, h, :] = piece
        else:
            cache_ref[:, :, h, :] = piece


def _project_cache(xn, w_main, gain, w_section, normed, cache, layer, sample):
    rows, d = xn.shape
    _, b, t, heads, _ = cache.shape
    hb = _pick(heads, CACHE_HEADS, F32_ROWS)
    tn = hb * HEAD_DIM
    nj = heads // hb
    n = b * t
    if sample:
        tm = n
        x_spec = pl.BlockSpec((n, d), lambda i, j: ((rows - n) // n, 0))
        c_spec = pl.BlockSpec((None, b, t, hb, HEAD_DIM), lambda i, j: (layer, 0, 0, j, 0))
    else:
        assert b == 1
        tm = _pick(n, 1024, BF16_ROWS)
        x_spec = pl.BlockSpec((tm, d), lambda i, j: (i, 0))
        c_spec = pl.BlockSpec((None, None, tm, hb, HEAD_DIM), lambda i, j: (layer, 0, i, j, 0))
    return pl.pallas_call(
        functools.partial(_proj_cache_kernel, normed=normed),
        grid=(n // tm, nj),
        in_specs=[x_spec,
                  pl.BlockSpec((d, tn), lambda i, j: (0, w_section * nj + j)),
                  pl.BlockSpec((1, HEAD_DIM), lambda i, j: (0, 0)),
                  pl.BlockSpec(memory_space=pl.ANY)],
        out_specs=[c_spec, pl.BlockSpec((tm, tn), lambda i, j: (i, j))],
        out_shape=[jax.ShapeDtypeStruct(cache.shape, F32),
                   jax.ShapeDtypeStruct((n, heads * HEAD_DIM), BF16)],
        input_output_aliases={3: 0},
        compiler_params=_params("parallel", "arbitrary"),
        name="proj_cache",
    )(xn, w_main, gain.reshape(1, HEAD_DIM), cache)


def _logf_kernel(wf_ref, xn_ref, b_ref, o_ref):
    z = _dot_nt(wf_ref[...], xn_ref[...]) + b_ref[...]
    o_ref[...] = _log_sigmoid(z)


def _log_forget(xn, wf_t, b_forget):
    rows, d = xn.shape
    heads = wf_t.shape[0]
    tt = _pick(rows, 1664, LANES)
    return pl.pallas_call(
        _logf_kernel,
        grid=(rows // tt,),
        in_specs=[pl.BlockSpec((heads, d), lambda i: (0, 0)),
                  pl.BlockSpec((tt, d), lambda i: (i, 0)),
                  pl.BlockSpec((heads, 1), lambda i: (0, 0))],
        out_specs=pl.BlockSpec((heads, tt), lambda i: (0, i)),
        out_shape=jax.ShapeDtypeStruct((heads, rows), F32),
        compiler_params=_params("parallel"),
        name="log_forget",
    )(wf_t, xn, b_forget.reshape(heads, 1))


def _split3(x):
    hi = x.astype(BF16)
    r1 = x - hi.astype(F32)
    mid = r1.astype(BF16)
    lo = (r1 - mid.astype(F32)).astype(BF16)
    return hi, mid, lo


def _cumsum_kernel(x_ref, o_ref):
    rows, length = x_ref.shape
    r = lax.broadcasted_iota(jnp.int32, (LANES, LANES), 0)
    c = lax.broadcasted_iota(jnp.int32, (LANES, LANES), 1)
    tri = jnp.where(r <= c, 1.0, 0.0).astype(BF16)

    def body(n, carry):
        off = pl.multiple_of(n * LANES, LANES)
        hi, mid, lo = _split3(x_ref[:, pl.ds(off, LANES)])
        y = (_dot(hi, tri) + _dot(mid, tri)) + _dot(lo, tri) + carry
        o_ref[:, pl.ds(off, LANES)] = y
        return y[:, LANES - 1:LANES]

    lax.fori_loop(0, length // LANES, body, jnp.zeros((rows, 1), F32))


def _cumsum_lanes(x):
    return pl.pallas_call(
        _cumsum_kernel,
        out_shape=jax.ShapeDtypeStruct(x.shape, F32),
        compiler_params=pltpu.CompilerParams(vmem_limit_bytes=VMEM_LIMIT),
        name="cumsum",
    )(x)


ATTN_BLOCK = 1024
SUFFIX_BLOCK = 256
FOX_ROW_CHUNKS = 4


def _tri_masks(rows, cols):
    row = lax.broadcasted_iota(jnp.int32, (rows, cols), 0)
    col = lax.broadcasted_iota(jnp.int32, (rows, cols), 1)
    return row, col


def _neg_suffix(blk):
    r, c = _tri_masks(2 * blk, blk)
    r = jnp.where(r >= blk, r - blk, r)
    return jnp.where(r >= c, -1.0, 0.0).astype(BF16)


def _sb_block(q, k, v, neg_suffix, state, valid):
    acc, carry = state
    sub = neg_suffix.shape[1]
    zs, tails = [], []
    for c in range(k.shape[0] // sub):
        cols = slice(c * sub, (c + 1) * sub)
        z = _dot_nt(q, k[cols, :])
        softplus = jnp.maximum(z, 0.0) + jnp.log(1.0 + jnp.exp2(jnp.abs(z) * -LOG2E))
        if valid is not None:
            softplus = jnp.where(valid[:, cols], softplus, 0.0)
        hi = softplus.astype(BF16)
        lo = (softplus - hi.astype(F32)).astype(BF16)
        zs.append(z)
        tails.append(_dot(jnp.concatenate([hi, lo], axis=1), neg_suffix))
    for c in reversed(range(len(zs))):
        cols = slice(c * sub, (c + 1) * sub)
        tail = tails[c] + carry
        a = jnp.exp(zs[c] + tail)
        if valid is not None:
            a = jnp.where(valid[:, cols], a, 0.0)
        acc = acc + _dot(a.astype(BF16), v[cols, :])
        carry = tail[:, 0:1]
    return acc, carry


def _causal_sweep(i, blk, step, state, strict):
    row, col = _tri_masks(blk, blk)
    state = step(i, state, col < row if strict else col <= row)
    return lax.fori_loop(0, i, lambda n, st: step(i - 1 - n, st, None), state)


def _sb_prompt_kernel(q_ref, k_ref, v_ref, o_in_ref, o_ref, *, blk, sub):
    del o_in_ref
    neg_suffix = _neg_suffix(sub)
    q = q_ref[...]

    def step(j, state, valid):
        rows = pl.ds(pl.multiple_of(j * blk, blk), blk)
        return _sb_block(q, k_ref[rows, :], v_ref[rows, :], neg_suffix, state, valid)

    state = (jnp.zeros((blk, HEAD_DIM), F32), jnp.zeros((blk, 1), F32))
    acc, _ = _causal_sweep(pl.program_id(1), blk, step, state, strict=True)
    o_ref[...] = acc.astype(o_ref.dtype)


def _fox_prompt_kernel(q_ref, k_ref, v_ref, ck_ref, o_in_ref, o_ref, *, blk, chunks):
    del o_in_ref
    ones = jnp.ones((blk, HEAD_DIM), BF16)
    tq = blk // chunks

    def step(j, state, valid):
        rows = pl.ds(pl.multiple_of(j * blk, blk), blk)
        k = k_ref[rows, :]
        ck = ck_ref[:, rows]
        v1 = jnp.concatenate([v_ref[rows, :], ones], axis=1)
        new = []
        for c, (m, acc) in enumerate(state):
            s = _dot_nt(q_ref[c * tq:(c + 1) * tq, :], k) - ck
            if valid is not None:
                s = jnp.where(valid[c * tq:(c + 1) * tq], s, -jnp.inf)
            m_new = jnp.maximum(m, jnp.max(s, axis=1, keepdims=True))
            p = jnp.exp(s - m_new).astype(BF16)
            new.append((m_new, jnp.exp(m - m_new) * acc + _dot(p, v1)))
        return tuple(new)

    state = tuple((jnp.full((tq, 1), -jnp.inf, F32), jnp.zeros((tq, 2 * HEAD_DIM), F32)) for _ in range(chunks))
    state = _causal_sweep(pl.program_id(1), blk, step, state, strict=False)
    for c, (_, acc) in enumerate(state):
        o_ref[c * tq:(c + 1) * tq, :] = (acc[:, :HEAD_DIM] / acc[:, HEAD_DIM:]).astype(o_ref.dtype)


def _prompt_mixers(qq, kv, cum_k, o_a, o_b, seq, heads):
    sub = _pick(seq, SUFFIX_BLOCK, LANES)
    blk = _pick(seq, ATTN_BLOCK, sub)
    chunks = FOX_ROW_CHUNKS if blk % (FOX_ROW_CHUNKS * BF16_ROWS) == 0 else 1
    grid = (heads, seq // blk)
    q_blk = (blk, HEAD_DIM)
    kv_spec = pl.BlockSpec((seq, HEAD_DIM), lambda h, i: (0, h))
    any_spec = pl.BlockSpec(memory_space=pl.ANY)
    out_spec = pl.BlockSpec(q_blk, lambda h, i: (i, h))
    out_shape = jax.ShapeDtypeStruct(o_a.shape, BF16)

    o_a = pl.pallas_call(
        functools.partial(_sb_prompt_kernel, blk=blk, sub=sub),
        grid=grid,
        in_specs=[pl.BlockSpec(q_blk, lambda h, i: (i, h)), kv_spec, kv_spec, any_spec],
        out_specs=out_spec,
        out_shape=out_shape,
        input_output_aliases={3: 0},
        compiler_params=_params("parallel", "arbitrary"),
        name="sb_prompt",
    )(qq, kv[0], kv[1], o_a)

    o_b = pl.pallas_call(
        functools.partial(_fox_prompt_kernel, blk=blk, chunks=chunks),
        grid=grid,
        in_specs=[pl.BlockSpec(q_blk, lambda h, i: (i, heads + h)), kv_spec, kv_spec,
                  pl.BlockSpec((None, 1, seq), lambda h, i: (h, 0, 0)), any_spec],
        out_specs=out_spec,
        out_shape=out_shape,
        input_output_aliases={4: 0},
        compiler_params=_params("parallel", "arbitrary"),
        name="fox_prompt",
    )(qq, kv[2], kv[3], cum_k, o_b)
    return o_a, o_b


def _sample_kernel(qa_ref, qb_ref, kna_ref, vna_ref, knb_ref, vnb_ref,
                   kca_ref, vca_ref, kcb_ref, vcb_ref, cum_ref, oa_in_ref, ob_in_ref, oa_ref, ob_ref, *, past, tn):
    del oa_in_ref, ob_in_ref
    pad = jnp.zeros((LANES - tn, HEAD_DIM), BF16)
    row, col = _tri_masks(tn, LANES)
    neg_suffix = _neg_suffix(LANES)
    n_chunks = past // LANES

    q = qa_ref[...]
    k_new = jnp.concatenate([kna_ref[...], pad], axis=0)
    v_new = jnp.concatenate([vna_ref[...], pad], axis=0)
    state = (jnp.zeros((tn, HEAD_DIM), F32), jnp.zeros((tn, 1), F32))
    state = _sb_block(q, k_new, v_new, neg_suffix, state, col < row)
    head = pl.program_id(1) % kca_ref.shape[1]
    for c in reversed(range(n_chunks)):
        k = kca_ref[c * LANES:(c + 1) * LANES, head, :].astype(BF16)
        v = vca_ref[c * LANES:(c + 1) * LANES, head, :].astype(BF16)
        state = _sb_block(q, k, v, neg_suffix, state, None)
    oa_ref[...] = state[0].astype(oa_ref.dtype)

    q = qb_ref[...]
    k_new = jnp.concatenate([knb_ref[...], pad], axis=0)
    v_new = jnp.concatenate([vnb_ref[...], pad], axis=0)
    s_new = _dot_nt(q, k_new) - cum_ref[:, past:past + LANES]
    s_new = jnp.where(col <= row, s_new, -jnp.inf)
    s_old = _dot_nt(q, kcb_ref[:, head, :].astype(BF16)) - cum_ref[:, 0:past]
    m = jnp.maximum(jnp.max(s_new, axis=1, keepdims=True), jnp.max(s_old, axis=1, keepdims=True))
    p_new = jnp.exp(s_new - m)
    p_old = jnp.exp(s_old - m)
    l = jnp.sum(p_new, axis=1, keepdims=True) + jnp.sum(p_old, axis=1, keepdims=True)
    acc = _dot(p_new.astype(BF16), v_new) + _dot(p_old.astype(BF16), vcb_ref[:, head, :].astype(BF16))
    ob_ref[...] = (acc / l).astype(ob_ref.dtype)


def _sample_mixers(qq, kv_new, caches, layer, cum_s, o_a, o_b, heads, n_streams, tn):
    past = caches[0].shape[2]
    row0 = o_a.shape[0] // tn - n_streams
    new_blk = (tn, HEAD_DIM)
    resident = _pick(heads, CACHE_HEADS, F32_ROWS)
    q_specs = [pl.BlockSpec(new_blk, lambda b, h: (row0 + b, h)),
               pl.BlockSpec(new_blk, lambda b, h: (row0 + b, heads + h))]
    new_spec = pl.BlockSpec(new_blk, lambda b, h: (b, h))
    cache_spec = pl.BlockSpec((None, None, past, resident, HEAD_DIM), lambda b, h: (layer, b, 0, h // resident, 0))
    any_spec = pl.BlockSpec(memory_space=pl.ANY)
    out_spec = pl.BlockSpec(new_blk, lambda b, h: (row0 + b, h))
    out_shape = jax.ShapeDtypeStruct(o_a.shape, BF16)
    return pl.pallas_call(
        functools.partial(_sample_kernel, past=past, tn=tn),
        grid=(n_streams, heads),
        in_specs=q_specs + [new_spec] * 4 + [cache_spec] * 4
                 + [pl.BlockSpec((None, None, 1, past + LANES), lambda b, h: (b, h, 0, 0)), any_spec, any_spec],
        out_specs=[out_spec, out_spec],
        out_shape=[out_shape, out_shape],
        input_output_aliases={11: 0, 12: 1},
        compiler_params=_params("parallel", "arbitrary"),
        name="sample_mixers",
    )(qq, qq, *kv_new, *caches, cum_s, o_a, o_b)


def _mix_kernel(oa_ref, ob_ref, xn_ref, wpa_ref, wpb_ref, wga_ref, wgb_ref, ba_ref, bb_ref, o_ref):
    xn = xn_ref[...]
    g_a = jax.nn.sigmoid(_dot(xn, wga_ref[...]) + ba_ref[...])
    g_b = jax.nn.sigmoid(_dot(xn, wgb_ref[...]) + bb_ref[...])
    mix = g_a * _dot(oa_ref[...], wpa_ref[...]) + g_b * _dot(ob_ref[...], wpb_ref[...])
    o_ref[...] = mix.astype(o_ref.dtype)


def _gated_mix(o_a, o_b, xn, w_proj_a, w_proj_b, w_gate, b_gate):
    rows, d = xn.shape
    width = o_a.shape[1]
    tm = _pick(rows, 832, BF16_ROWS)
    tn = _pick(d, 256, LANES)
    nb = d // tn
    b2 = b_gate.reshape(1, 2 * d)
    return pl.pallas_call(
        _mix_kernel,
        grid=(rows // tm, nb),
        in_specs=[pl.BlockSpec((tm, width), lambda i, j: (i, 0)),
                  pl.BlockSpec((tm, width), lambda i, j: (i, 0)),
                  pl.BlockSpec((tm, d), lambda i, j: (i, 0)),
                  pl.BlockSpec((width, tn), lambda i, j: (0, j)),
                  pl.BlockSpec((width, tn), lambda i, j: (0, j)),
                  pl.BlockSpec((d, tn), lambda i, j: (0, j)),
                  pl.BlockSpec((d, tn), lambda i, j: (0, nb + j)),
                  pl.BlockSpec((1, tn), lambda i, j: (0, j)),
                  pl.BlockSpec((1, tn), lambda i, j: (0, nb + j))],
        out_specs=pl.BlockSpec((tm, tn), lambda i, j: (i, j)),
        out_shape=jax.ShapeDtypeStruct((rows, d), BF16),
        compiler_params=_params("parallel", "arbitrary"),
        name="gated_mix",
    )(o_a, o_b, xn, w_proj_a, w_proj_b, w_gate, w_gate, b2, b2)


def _residual_matmul_kernel(a_ref, w_ref, r_ref, o_ref):
    o_ref[...] = r_ref[...] + _dot(a_ref[...], w_ref[...])


def _residual_matmul(a, w, resid, tm_target, tn_target):
    rows, k = a.shape
    n = w.shape[1]
    tm = _pick(rows, tm_target, BF16_ROWS)
    tn = _pick(n, tn_target, LANES)
    return pl.pallas_call(
        _residual_matmul_kernel,
        grid=(rows // tm, n // tn),
        in_specs=[pl.BlockSpec((tm, k), lambda i, j: (i, 0)),
                  pl.BlockSpec((k, tn), lambda i, j: (0, j)),
                  pl.BlockSpec((tm, tn), lambda i, j: (i, j))],
        out_specs=pl.BlockSpec((tm, tn), lambda i, j: (i, j)),
        out_shape=jax.ShapeDtypeStruct((rows, n), F32),
        compiler_params=_params("parallel", "arbitrary"),
        name="residual_matmul",
    )(a, w, resid)


def _swiglu_up_kernel(x_ref, wg_ref, wu_ref, o_ref):
    x = x_ref[...]
    g = _dot(x, wg_ref[...])
    o_ref[...] = (g * jax.nn.sigmoid(g) * _dot(x, wu_ref[...])).astype(o_ref.dtype)


def _swiglu_up(hn, w_gate, w_up):
    rows, d = hn.shape
    dff = w_gate.shape[1]
    tm = _pick(rows, 1040, BF16_ROWS)
    tn = _pick(dff, 256, LANES)
    return pl.pallas_call(
        _swiglu_up_kernel,
        grid=(rows // tm, dff // tn),
        in_specs=[pl.BlockSpec((tm, d), lambda i, j: (i, 0)),
                  pl.BlockSpec((d, tn), lambda i, j: (0, j)),
                  pl.BlockSpec((d, tn), lambda i, j: (0, j))],
        out_specs=pl.BlockSpec((tm, tn), lambda i, j: (i, j)),
        out_shape=jax.ShapeDtypeStruct((rows, dff), BF16),
        compiler_params=_params("parallel", "arbitrary"),
        name="swiglu_up",
    )(hn, w_gate, w_up)


def kernel(x_prompt, x_sample, cache_sb_k, cache_sb_v, cache_fox_k, cache_fox_v, cache_fox_logf, norm_attn, w_in, b_forget, q_norm, k_norm, w_proj_a, w_proj_b, w_gate_br, b_gate_br, w_out, norm_ffn, w_ffn_gate, w_ffn_up, w_ffn_down):
    batch, seq, d = x_prompt.shape
    n_streams, tn, _ = x_sample.shape
    depth, _, past, heads, _ = cache_sb_k.shape
    assert batch == 1 and seq % tn == 0 and tn % BF16_ROWS == 0 and past % LANES == 0
    width = heads * HEAD_DIM
    n_new = n_streams * tn
    rows = seq + n_new

    x = jnp.concatenate([x_prompt.reshape(seq, d), x_sample.reshape(n_new, d)], axis=0)
    caches = [cache_sb_k, cache_sb_v, cache_fox_k, cache_fox_v]
    logf_past = jnp.transpose(cache_fox_logf, (0, 1, 3, 2)).astype(F32)

    p_cache = [jnp.zeros((depth, batch, seq, heads, HEAD_DIM), F32) for _ in range(4)]
    s_cache = [jnp.zeros((depth, n_streams, tn, heads, HEAD_DIM), F32) for _ in range(4)]
    p_logf, s_logf = [], []
    kv_sections = ((1, False), (2, False), (4, True), (5, False))
    for l in range(depth):
        w_main = w_in[l].astype(BF16)
        wf_t = jnp.transpose(w_in[l, :, 6 * width:]).astype(BF16)

        xn = _rmsnorm(x, norm_attn[l])
        qq = _project_q(xn, w_main, q_norm[l], width)
        kv_p, kv_s = [], []
        for n, (section, normed) in enumerate(kv_sections):
            p_cache[n], kb = _project_cache(xn, w_main, k_norm[l], section, normed, p_cache[n], l, sample=False)
            s_cache[n], ks = _project_cache(xn, w_main, k_norm[l], section, normed, s_cache[n], l, sample=True)
            kv_p.append(kb)
            kv_s.append(ks)
        logf_t = _log_forget(xn, wf_t, b_forget[l])

        cum_p = _cumsum_lanes(logf_t[:, :seq]).reshape(heads, 1, seq)
        logf_new = jnp.transpose(logf_t[:, seq:].reshape(heads, n_streams, tn), (1, 0, 2))
        logf_all = jnp.concatenate(
            [logf_past[l], logf_new, jnp.zeros((n_streams, heads, LANES - tn), F32)], axis=2)
        cum_s = _cumsum_lanes(logf_all.reshape(n_streams * heads, past + LANES))
        cum_s = cum_s.reshape(n_streams, heads, 1, past + LANES)

        o_a = jnp.zeros((rows, width), BF16)
        o_b = jnp.zeros((rows, width), BF16)
        o_a, o_b = _sample_mixers(qq, kv_s, caches, l, cum_s, o_a, o_b, heads, n_streams, tn)
        o_a, o_b = _prompt_mixers(qq, kv_p, cum_p, o_a, o_b, seq, heads)

        mix = _gated_mix(o_a, o_b, xn, w_proj_a[l].astype(BF16), w_proj_b[l].astype(BF16),
                         w_gate_br[l].astype(BF16), b_gate_br[l])
        h = _residual_matmul(mix, w_out[l].astype(BF16), x, 1040, 512)
        hn = _rmsnorm(h, norm_ffn[l])
        act = _swiglu_up(hn, w_ffn_gate[l].astype(BF16), w_ffn_up[l].astype(BF16))
        x = _residual_matmul(act, w_ffn_down[l].astype(BF16), h, 640, 256)

        logf_rows = jnp.transpose(logf_t)
        p_logf.append(logf_rows[:seq].reshape(batch, seq, heads))
        s_logf.append(logf_rows[seq:].reshape(n_streams, tn, heads))

    return (x[:seq].reshape(batch, seq, d), x[seq:].reshape(n_streams, tn, d),
            *p_cache, jnp.stack(p_logf), *s_cache, jnp.stack(s_logf))
```

```python
import functools

import jax
import jax.numpy as jnp
from jax import lax
from jax.experimental import pallas as pl
from jax.experimental.pallas import tpu as pltpu

EPS = 1e-6
LOG2E = 1.4426950408889634
HEAD_DIM = 128
LANES = 128
BF16_ROWS = 16
F32_ROWS = 8
CACHE_HEADS = 8
VMEM_LIMIT = 56 * 1024 * 1024
F32 = jnp.float32
BF16 = jnp.bfloat16


def _pick(n, target, mult):
    best = None
    for d in range(mult, min(n, target) + 1, mult):
        if n % d == 0:
            best = d
    return best if best is not None else n


def _params(*sem):
    return pltpu.CompilerParams(dimension_semantics=sem, vmem_limit_bytes=VMEM_LIMIT)


def _dot(a, b):
    return jnp.dot(a, b, preferred_element_type=F32)


def _dot_nt(a, b):
    return lax.dot_general(a, b, (((1,), (1,)), ((), ())), preferred_element_type=F32)


def _log_sigmoid(x):
    return jnp.minimum(x, 0.0) - jnp.log1p(jnp.exp(-jnp.abs(x)))


def _rmsnorm_kernel(x_ref, g_ref, o_ref):
    x = x_ref[...]
    y = x * lax.rsqrt(jnp.mean(x * x, axis=-1, keepdims=True) + EPS)
    o_ref[...] = (y * g_ref[...]).astype(o_ref.dtype)


def _rmsnorm(x, g):
    rows, d = x.shape
    tm = _pick(rows, 640, BF16_ROWS)
    return pl.pallas_call(
        _rmsnorm_kernel,
        grid=(rows // tm,),
        in_specs=[pl.BlockSpec((tm, d), lambda i: (i, 0)), pl.BlockSpec((1, d), lambda i: (0, 0))],
        out_specs=pl.BlockSpec((tm, d), lambda i: (i, 0)),
        out_shape=jax.ShapeDtypeStruct((rows, d), BF16),
        compiler_params=_params("parallel"),
        name="rmsnorm",
    )(x, g.reshape(1, d))


def _head_rmsnorm(acc, gain):
    outs = []
    for c in range(acc.shape[1] // HEAD_DIM):
        blk = acc[:, c * HEAD_DIM:(c + 1) * HEAD_DIM]
        y = blk * lax.rsqrt(jnp.mean(blk * blk, axis=-1, keepdims=True) + EPS)
        outs.append(y * gain)
    return outs[0] if len(outs) == 1 else jnp.concatenate(outs, axis=1)


def _proj_q_kernel(xn_ref, w_ref, gain_ref, o_ref, *, nb, scale):
    acc = _dot(xn_ref[...], w_ref[...])

    @pl.when(pl.program_id(1) < nb)
    def _():
        o_ref[...] = (acc * scale).astype(o_ref.dtype)

    @pl.when(pl.program_id(1) >= nb)
    def _():
        o_ref[...] = (_head_rmsnorm(acc, gain_ref[...]) * scale).astype(o_ref.dtype)


def _project_q(xn, w_main, layer, q_gain, width):
    rows, d = xn.shape
    tm = _pick(rows, 1040, BF16_ROWS)
    tn = _pick(width, 512, LANES)
    nb = width // tn
    return pl.pallas_call(
        functools.partial(_proj_q_kernel, nb=nb, scale=HEAD_DIM ** -0.5),
        grid=(rows // tm, 2 * nb),
        in_specs=[pl.BlockSpec((tm, d), lambda i, j: (i, 0)),
                  pl.BlockSpec((None, d, tn), lambda i, j: (layer, 0, jnp.where(j < nb, j, j + 2 * nb))),
                  pl.BlockSpec((1, HEAD_DIM), lambda i, j: (0, 0))],
        out_specs=pl.BlockSpec((tm, tn), lambda i, j: (i, j)),
        out_shape=jax.ShapeDtypeStruct((rows, 2 * width), BF16),
        compiler_params=_params("parallel", "arbitrary"),
        name="proj_q",
    )(xn, w_main, q_gain.reshape(1, HEAD_DIM))


def _proj_cache_kernel(xn_ref, w_ref, gain_ref, cache_in_ref, cache_ref, ob_ref, *, normed):
    del cache_in_ref
    y = _dot(xn_ref[...], w_ref[...])
    if normed:
        y = _head_rmsnorm(y, gain_ref[...])
    ob_ref[...] = y.astype(ob_ref.dtype)
    lead = cache_ref.shape[:-2]
    for h in range(cache_ref.shape[-2]):
        piece = y[:, h * HEAD_DIM:(h + 1) * HEAD_DIM].reshape(lead + (HEAD_DIM,))
        if len(lead) == 1:
            cache_ref[:, h, :] = piece
        else:
            cache_ref[:, :, h, :] = piece


def _project_cache(xn, w_main, gain, w_section, normed, cache, layer, sample):
    rows, d = xn.shape
    _, b, t, heads, _ = cache.shape
    hb = _pick(heads, CACHE_HEADS, F32_ROWS)
    tn = hb * HEAD_DIM
    nj = heads // hb
    n = b * t
    if sample:
        tm = n
        x_spec = pl.BlockSpec((n, d), lambda i, j: ((rows - n) // n, 0))
        c_spec = pl.BlockSpec((None, b, t, hb, HEAD_DIM), lambda i, j: (layer, 0, 0, j, 0))
    else:
        assert b == 1
        tm = _pick(n, 1024, BF16_ROWS)
        x_spec = pl.BlockSpec((tm, d), lambda i, j: (i, 0))
        c_spec = pl.BlockSpec((None, None, tm, hb, HEAD_DIM), lambda i, j: (layer, 0, i, j, 0))
    return pl.pallas_call(
        functools.partial(_proj_cache_kernel, normed=normed),
        grid=(n // tm, nj),
        in_specs=[x_spec,
                  pl.BlockSpec((None, d, tn), lambda i, j: (layer, 0, w_section * nj + j)),
                  pl.BlockSpec((1, HEAD_DIM), lambda i, j: (0, 0)),
                  pl.BlockSpec(memory_space=pl.ANY)],
        out_specs=[c_spec, pl.BlockSpec((tm, tn), lambda i, j: (i, j))],
        out_shape=[jax.ShapeDtypeStruct(cache.shape, F32),
                   jax.ShapeDtypeStruct((n, heads * HEAD_DIM), BF16)],
        input_output_aliases={3: 0},
        compiler_params=_params("parallel", "arbitrary"),
        name="proj_cache",
    )(xn, w_main, gain.reshape(1, HEAD_DIM), cache)


def _logf_kernel(wf_ref, xn_ref, b_ref, o_ref):
    z = _dot_nt(wf_ref[...], xn_ref[...]) + b_ref[...]
    o_ref[...] = _log_sigmoid(z)


def _log_forget(xn, wf_t, b_forget):
    rows, d = xn.shape
    heads = wf_t.shape[0]
    tt = _pick(rows, 1664, LANES)
    return pl.pallas_call(
        _logf_kernel,
        grid=(rows // tt,),
        in_specs=[pl.BlockSpec((heads, d), lambda i: (0, 0)),
                  pl.BlockSpec((tt, d), lambda i: (i, 0)),
                  pl.BlockSpec((heads, 1), lambda i: (0, 0))],
        out_specs=pl.BlockSpec((heads, tt), lambda i: (0, i)),
        out_shape=jax.ShapeDtypeStruct((heads, rows), F32),
        compiler_params=_params("parallel"),
        name="log_forget",
    )(wf_t, xn, b_forget.reshape(heads, 1))


def _split3(x):
    hi = x.astype(BF16)
    r1 = x - hi.astype(F32)
    mid = r1.astype(BF16)
    lo = (r1 - mid.astype(F32)).astype(BF16)
    return hi, mid, lo


def _cumsum_kernel(x_ref, o_ref):
    rows, length = x_ref.shape
    r = lax.broadcasted_iota(jnp.int32, (LANES, LANES), 0)
    c = lax.broadcasted_iota(jnp.int32, (LANES, LANES), 1)
    tri = jnp.where(r <= c, 1.0, 0.0).astype(BF16)

    def body(n, carry):
        off = pl.multiple_of(n * LANES, LANES)
        hi, mid, lo = _split3(x_ref[:, pl.ds(off, LANES)])
        y = (_dot(hi, tri) + _dot(mid, tri)) + _dot(lo, tri) + carry
        o_ref[:, pl.ds(off, LANES)] = y
        return y[:, LANES - 1:LANES]

    lax.fori_loop(0, length // LANES, body, jnp.zeros((rows, 1), F32))


def _cumsum_lanes(x):
    return pl.pallas_call(
        _cumsum_kernel,
        out_shape=jax.ShapeDtypeStruct(x.shape, F32),
        compiler_params=pltpu.CompilerParams(vmem_limit_bytes=VMEM_LIMIT),
        name="cumsum",
    )(x)


ATTN_BLOCK = 1024
SUFFIX_BLOCK = 256
FOX_ROW_CHUNKS = 4


def _tri_masks(rows, cols):
    row = lax.broadcasted_iota(jnp.int32, (rows, cols), 0)
    col = lax.broadcasted_iota(jnp.int32, (rows, cols), 1)
    return row, col


def _neg_suffix(blk):
    r, c = _tri_masks(2 * blk, blk)
    r = jnp.where(r >= blk, r - blk, r)
    return jnp.where(r >= c, -1.0, 0.0).astype(BF16)


def _sb_block(q, k, v, neg_suffix, state, valid):
    acc, carry = state
    sub = neg_suffix.shape[1]
    zs, tails = [], []
    for c in range(k.shape[0] // sub):
        cols = slice(c * sub, (c + 1) * sub)
        z = _dot_nt(q, k[cols, :])
        softplus = jnp.maximum(z, 0.0) + jnp.log(1.0 + jnp.exp2(jnp.abs(z) * -LOG2E))
        if valid is not None:
            softplus = jnp.where(valid[:, cols], softplus, 0.0)
        hi = softplus.astype(BF16)
        lo = (softplus - hi.astype(F32)).astype(BF16)
        zs.append(z)
        tails.append(_dot(jnp.concatenate([hi, lo], axis=1), neg_suffix))
    for c in reversed(range(len(zs))):
        cols = slice(c * sub, (c + 1) * sub)
        tail = tails[c] + carry
        a = jnp.exp(zs[c] + tail)
        if valid is not None:
            a = jnp.where(valid[:, cols], a, 0.0)
        acc = acc + _dot(a.astype(BF16), v[cols, :])
        carry = tail[:, 0:1]
    return acc, carry


def _causal_sweep(i, blk, step, state, strict):
    row, col = _tri_masks(blk, blk)
    state = step(i, state, col < row if strict else col <= row)
    return lax.fori_loop(0, i, lambda n, st: step(i - 1 - n, st, None), state)


def _sb_prompt_kernel(q_ref, k_ref, v_ref, o_in_ref, o_ref, *, blk, sub):
    del o_in_ref
    neg_suffix = _neg_suffix(sub)
    q = q_ref[...]

    def step(j, state, valid):
        rows = pl.ds(pl.multiple_of(j * blk, blk), blk)
        return _sb_block(q, k_ref[rows, :], v_ref[rows, :], neg_suffix, state, valid)

    state = (jnp.zeros((blk, HEAD_DIM), F32), jnp.zeros((blk, 1), F32))
    acc, _ = _causal_sweep(pl.program_id(1), blk, step, state, strict=True)
    o_ref[...] = acc.astype(o_ref.dtype)


def _fox_prompt_kernel(q_ref, k_ref, v_ref, ck_ref, o_in_ref, o_ref, *, blk, chunks):
    del o_in_ref
    ones = jnp.ones((blk, HEAD_DIM), BF16)
    tq = blk // chunks

    def step(j, state, valid):
        rows = pl.ds(pl.multiple_of(j * blk, blk), blk)
        k = k_ref[rows, :]
        ck = ck_ref[:, rows]
        v1 = jnp.concatenate([v_ref[rows, :], ones], axis=1)
        new = []
        for c, (m, acc) in enumerate(state):
            s = _dot_nt(q_ref[c * tq:(c + 1) * tq, :], k) - ck
            if valid is not None:
                s = jnp.where(valid[c * tq:(c + 1) * tq], s, -jnp.inf)
            m_new = jnp.maximum(m, jnp.max(s, axis=1, keepdims=True))
            p = jnp.exp(s - m_new).astype(BF16)
            new.append((m_new, jnp.exp(m - m_new) * acc + _dot(p, v1)))
        return tuple(new)

    state = tuple((jnp.full((tq, 1), -jnp.inf, F32), jnp.zeros((tq, 2 * HEAD_DIM), F32)) for _ in range(chunks))
    state = _causal_sweep(pl.program_id(1), blk, step, state, strict=False)
    for c, (_, acc) in enumerate(state):
        o_ref[c * tq:(c + 1) * tq, :] = (acc[:, :HEAD_DIM] / acc[:, HEAD_DIM:]).astype(o_ref.dtype)


def _prompt_mixers(qq, kv, cum_k, o_a, o_b, seq, heads):
    sub = _pick(seq, SUFFIX_BLOCK, LANES)
    blk = _pick(seq, ATTN_BLOCK, sub)
    chunks = FOX_ROW_CHUNKS if blk % (FOX_ROW_CHUNKS * BF16_ROWS) == 0 else 1
    grid = (heads, seq // blk)
    q_blk = (blk, HEAD_DIM)
    kv_spec = pl.BlockSpec((seq, HEAD_DIM), lambda h, i: (0, h))
    any_spec = pl.BlockSpec(memory_space=pl.ANY)
    out_spec = pl.BlockSpec(q_blk, lambda h, i: (i, h))
    out_shape = jax.ShapeDtypeStruct(o_a.shape, BF16)

    o_a = pl.pallas_call(
        functools.partial(_sb_prompt_kernel, blk=blk, sub=sub),
        grid=grid,
        in_specs=[pl.BlockSpec(q_blk, lambda h, i: (i, h)), kv_spec, kv_spec, any_spec],
        out_specs=out_spec,
        out_shape=out_shape,
        input_output_aliases={3: 0},
        compiler_params=_params("parallel", "arbitrary"),
        name="sb_prompt",
    )(qq, kv[0], kv[1], o_a)

    o_b = pl.pallas_call(
        functools.partial(_fox_prompt_kernel, blk=blk, chunks=chunks),
        grid=grid,
        in_specs=[pl.BlockSpec(q_blk, lambda h, i: (i, heads + h)), kv_spec, kv_spec,
                  pl.BlockSpec((None, 1, seq), lambda h, i: (h, 0, 0)), any_spec],
        out_specs=out_spec,
        out_shape=out_shape,
        input_output_aliases={4: 0},
        compiler_params=_params("parallel", "arbitrary"),
        name="fox_prompt",
    )(qq, kv[2], kv[3], cum_k, o_b)
    return o_a, o_b


def _sample_kernel(qa_ref, qb_ref, kna_ref, vna_ref, knb_ref, vnb_ref,
                   kca_ref, vca_ref, kcb_ref, vcb_ref, cum_ref, oa_in_ref, ob_in_ref, oa_ref, ob_ref, *, past, tn):
    del oa_in_ref, ob_in_ref
    pad = jnp.zeros((LANES - tn, HEAD_DIM), BF16)
    row, col = _tri_masks(tn, LANES)
    neg_suffix = _neg_suffix(LANES)
    n_chunks = past // LANES

    q = qa_ref[...]
    k_new = jnp.concatenate([kna_ref[...], pad], axis=0)
    v_new = jnp.concatenate([vna_ref[...], pad], axis=0)
    state = (jnp.zeros((tn, HEAD_DIM), F32), jnp.zeros((tn, 1), F32))
    state = _sb_block(q, k_new, v_new, neg_suffix, state, col < row)
    head = pl.program_id(1) % kca_ref.shape[1]
    for c in reversed(range(n_chunks)):
        k = kca_ref[c * LANES:(c + 1) * LANES, head, :].astype(BF16)
        v = vca_ref[c * LANES:(c + 1) * LANES, head, :].astype(BF16)
        state = _sb_block(q, k, v, neg_suffix, state, None)
    oa_ref[...] = state[0].astype(oa_ref.dtype)

    q = qb_ref[...]
    k_new = jnp.concatenate([knb_ref[...], pad], axis=0)
    v_new = jnp.concatenate([vnb_ref[...], pad], axis=0)
    s_new = _dot_nt(q, k_new) - cum_ref[:, past:past + LANES]
    s_new = jnp.where(col <= row, s_new, -jnp.inf)
    s_old = _dot_nt(q, kcb_ref[:, head, :].astype(BF16)) - cum_ref[:, 0:past]
    m = jnp.maximum(jnp.max(s_new, axis=1, keepdims=True), jnp.max(s_old, axis=1, keepdims=True))
    p_new = jnp.exp(s_new - m)
    p_old = jnp.exp(s_old - m)
    l = jnp.sum(p_new, axis=1, keepdims=True) + jnp.sum(p_old, axis=1, keepdims=True)
    acc = _dot(p_new.astype(BF16), v_new) + _dot(p_old.astype(BF16), vcb_ref[:, head, :].astype(BF16))
    ob_ref[...] = (acc / l).astype(ob_ref.dtype)


def _sample_mixers(qq, kv_new, caches, layer, cum_s, o_a, o_b, heads, n_streams, tn):
    past = caches[0].shape[2]
    row0 = o_a.shape[0] // tn - n_streams
    new_blk = (tn, HEAD_DIM)
    resident = _pick(heads, CACHE_HEADS, F32_ROWS)
    q_specs = [pl.BlockSpec(new_blk, lambda b, h: (row0 + b, h)),
               pl.BlockSpec(new_blk, lambda b, h: (row0 + b, heads + h))]
    new_spec = pl.BlockSpec(new_blk, lambda b, h: (b, h))
    cache_spec = pl.BlockSpec((None, None, past, resident, HEAD_DIM), lambda b, h: (layer, b, 0, h // resident, 0))
    any_spec = pl.BlockSpec(memory_space=pl.ANY)
    out_spec = pl.BlockSpec(new_blk, lambda b, h: (row0 + b, h))
    out_shape = jax.ShapeDtypeStruct(o_a.shape, BF16)
    return pl.pallas_call(
        functools.partial(_sample_kernel, past=past, tn=tn),
        grid=(n_streams, heads),
        in_specs=q_specs + [new_spec] * 4 + [cache_spec] * 4
                 + [pl.BlockSpec((None, None, 1, past + LANES), lambda b, h: (b, h, 0, 0)), any_spec, any_spec],
        out_specs=[out_spec, out_spec],
        out_shape=[out_shape, out_shape],
        input_output_aliases={11: 0, 12: 1},
        compiler_params=_params("parallel", "arbitrary"),
        name="sample_mixers",
    )(qq, qq, *kv_new, *caches, cum_s, o_a, o_b)


def _mix_kernel(oa_ref, ob_ref, xn_ref, wpa_ref, wpb_ref, wga_ref, wgb_ref, ba_ref, bb_ref, o_ref):
    xn = xn_ref[...]
    g_a = jax.nn.sigmoid(_dot(xn, wga_ref[...]) + ba_ref[...])
    g_b = jax.nn.sigmoid(_dot(xn, wgb_ref[...]) + bb_ref[...])
    mix = g_a * _dot(oa_ref[...], wpa_ref[...]) + g_b * _dot(ob_ref[...], wpb_ref[...])
    o_ref[...] = mix.astype(o_ref.dtype)


def _gated_mix(o_a, o_b, xn, w_proj_a, w_proj_b, w_gate, layer, b_gate):
    rows, d = xn.shape
    width = o_a.shape[1]
    tm = _pick(rows, 832, BF16_ROWS)
    tn = _pick(d, 256, LANES)
    nb = d // tn
    b2 = b_gate.reshape(1, 2 * d)
    return pl.pallas_call(
        _mix_kernel,
        grid=(rows // tm, nb),
        in_specs=[pl.BlockSpec((tm, width), lambda i, j: (i, 0)),
                  pl.BlockSpec((tm, width), lambda i, j: (i, 0)),
                  pl.BlockSpec((tm, d), lambda i, j: (i, 0)),
                  pl.BlockSpec((None, width, tn), lambda i, j: (layer, 0, j)),
                  pl.BlockSpec((None, width, tn), lambda i, j: (layer, 0, j)),
                  pl.BlockSpec((None, d, tn), lambda i, j: (layer, 0, j)),
                  pl.BlockSpec((None, d, tn), lambda i, j: (layer, 0, nb + j)),
                  pl.BlockSpec((1, tn), lambda i, j: (0, j)),
                  pl.BlockSpec((1, tn), lambda i, j: (0, nb + j))],
        out_specs=pl.BlockSpec((tm, tn), lambda i, j: (i, j)),
        out_shape=jax.ShapeDtypeStruct((rows, d), BF16),
        compiler_params=_params("parallel", "arbitrary"),
        name="gated_mix",
    )(o_a, o_b, xn, w_proj_a, w_proj_b, w_gate, w_gate, b2, b2)


def _residual_matmul_kernel(a_ref, w_ref, r_ref, o_ref):
    o_ref[...] = r_ref[...] + _dot(a_ref[...], w_ref[...])


def _residual_matmul(a, w, layer, resid, tm_target, tn_target):
    rows, k = a.shape
    n = w.shape[2]
    tm = _pick(rows, tm_target, BF16_ROWS)
    tn = _pick(n, tn_target, LANES)
    return pl.pallas_call(
        _residual_matmul_kernel,
        grid=(rows // tm, n // tn),
        in_specs=[pl.BlockSpec((tm, k), lambda i, j: (i, 0)),
                  pl.BlockSpec((None, k, tn), lambda i, j: (layer, 0, j)),
                  pl.BlockSpec((tm, tn), lambda i, j: (i, j))],
        out_specs=pl.BlockSpec((tm, tn), lambda i, j: (i, j)),
        out_shape=jax.ShapeDtypeStruct((rows, n), F32),
        compiler_params=_params("parallel", "arbitrary"),
        name="residual_matmul",
    )(a, w, resid)


def _swiglu_up_kernel(x_ref, wg_ref, wu_ref, o_ref):
    x = x_ref[...]
    g = _dot(x, wg_ref[...])
    o_ref[...] = (g * jax.nn.sigmoid(g) * _dot(x, wu_ref[...])).astype(o_ref.dtype)


def _swiglu_up(hn, w_gate, w_up, layer):
    rows, d = hn.shape
    dff = w_gate.shape[2]
    tm = _pick(rows, 1040, BF16_ROWS)
    tn = _pick(dff, 256, LANES)
    return pl.pallas_call(
        _swiglu_up_kernel,
        grid=(rows // tm, dff // tn),
        in_specs=[pl.BlockSpec((tm, d), lambda i, j: (i, 0)),
                  pl.BlockSpec((None, d, tn), lambda i, j: (layer, 0, j)),
                  pl.BlockSpec((None, d, tn), lambda i, j: (layer, 0, j))],
        out_specs=pl.BlockSpec((tm, tn), lambda i, j: (i, j)),
        out_shape=jax.ShapeDtypeStruct((rows, dff), BF16),
        compiler_params=_params("parallel", "arbitrary"),
        name="swiglu_up",
    )(hn, w_gate, w_up)


def kernel(x_prompt, x_sample, cache_sb_k, cache_sb_v, cache_fox_k, cache_fox_v, cache_fox_logf, norm_attn, w_in, b_forget, q_norm, k_norm, w_proj_a, w_proj_b, w_gate_br, b_gate_br, w_out, norm_ffn, w_ffn_gate, w_ffn_up, w_ffn_down):
    batch, seq, d = x_prompt.shape
    n_streams, tn, _ = x_sample.shape
    depth, _, past, heads, _ = cache_sb_k.shape
    assert batch == 1 and seq % tn == 0 and tn % BF16_ROWS == 0 and past % LANES == 0
    width = heads * HEAD_DIM
    n_new = n_streams * tn
    rows = seq + n_new

    x = jnp.concatenate([x_prompt.reshape(seq, d), x_sample.reshape(n_new, d)], axis=0)
    caches = [cache_sb_k, cache_sb_v, cache_fox_k, cache_fox_v]
    logf_past = jnp.transpose(cache_fox_logf, (0, 1, 3, 2)).astype(F32)

    p_cache = [jnp.zeros((depth, batch, seq, heads, HEAD_DIM), F32) for _ in range(4)]
    s_cache = [jnp.zeros((depth, n_streams, tn, heads, HEAD_DIM), F32) for _ in range(4)]
    p_logf, s_logf = [], []
    kv_sections = ((1, False), (2, False), (4, True), (5, False))
    w_main = w_in.astype(BF16)
    wpa, wpb, wgb, wo = (w.astype(BF16) for w in (w_proj_a, w_proj_b, w_gate_br, w_out))
    wfg, wfu, wfd = (w.astype(BF16) for w in (w_ffn_gate, w_ffn_up, w_ffn_down))
    for l in range(depth):
        wf_t = jnp.transpose(w_in[l, :, 6 * width:]).astype(BF16)

        xn = _rmsnorm(x, norm_attn[l])
        qq = _project_q(xn, w_main, l, q_norm[l], width)
        kv_p, kv_s = [], []
        for n, (section, normed) in enumerate(kv_sections):
            p_cache[n], kb = _project_cache(xn, w_main, k_norm[l], section, normed, p_cache[n], l, sample=False)
            s_cache[n], ks = _project_cache(xn, w_main, k_norm[l], section, normed, s_cache[n], l, sample=True)
            kv_p.append(kb)
            kv_s.append(ks)
        logf_t = _log_forget(xn, wf_t, b_forget[l])

        cum_p = _cumsum_lanes(logf_t[:, :seq]).reshape(heads, 1, seq)
        logf_new = jnp.transpose(logf_t[:, seq:].reshape(heads, n_streams, tn), (1, 0, 2))
        logf_all = jnp.concatenate(
            [logf_past[l], logf_new, jnp.zeros((n_streams, heads, LANES - tn), F32)], axis=2)
        cum_s = _cumsum_lanes(logf_all.reshape(n_streams * heads, past + LANES))
        cum_s = cum_s.reshape(n_streams, heads, 1, past + LANES)

        o_a = jnp.zeros((rows, width), BF16)
        o_b = jnp.zeros((rows, width), BF16)
        o_a, o_b = _sample_mixers(qq, kv_s, caches, l, cum_s, o_a, o_b, heads, n_streams, tn)
        o_a, o_b = _prompt_mixers(qq, kv_p, cum_p, o_a, o_b, seq, heads)

        mix = _gated_mix(o_a, o_b, xn, wpa, wpb, wgb, l, b_gate_br[l])
        h = _residual_matmul(mix, wo, l, x, 1040, 512)
        hn = _rmsnorm(h, norm_ffn[l])
        act = _swiglu_up(hn, wfg, wfu, l)
        x = _residual_matmul(act, wfd, l, h, 640, 256)

        logf_rows = jnp.transpose(logf_t)
        p_logf.append(logf_rows[:seq].reshape(batch, seq, heads))
        s_logf.append(logf_rows[seq:].reshape(n_streams, tn, heads))

    return (x[:seq].reshape(batch, seq, d), x[seq:].reshape(n_streams, tn, d),
            *p_cache, jnp.stack(p_logf), *s_cache, jnp.stack(s_logf))
```

```python
import functools

import jax
import jax.numpy as jnp
from jax import lax
from jax.experimental import pallas as pl
from jax.experimental.pallas import tpu as pltpu

EPS = 1e-6
LOG2E = 1.4426950408889634
HEAD_DIM = 128
LANES = 128
BF16_ROWS = 16
F32_ROWS = 8
CACHE_HEADS = 8
VMEM_LIMIT = 56 * 1024 * 1024
F32 = jnp.float32
BF16 = jnp.bfloat16


def _pick(n, target, mult):
    best = None
    for d in range(mult, min(n, target) + 1, mult):
        if n % d == 0:
            best = d
    return best if best is not None else n


def _params(*sem):
    return pltpu.CompilerParams(dimension_semantics=sem, vmem_limit_bytes=VMEM_LIMIT)


def _dot(a, b):
    return jnp.dot(a, b, preferred_element_type=F32)


def _dot_nt(a, b):
    return lax.dot_general(a, b, (((1,), (1,)), ((), ())), preferred_element_type=F32)


def _log_sigmoid(x):
    return jnp.minimum(x, 0.0) - jnp.log1p(jnp.exp(-jnp.abs(x)))


def _rmsnorm_kernel(x_ref, g_ref, o_ref):
    x = x_ref[...]
    y = x * lax.rsqrt(jnp.mean(x * x, axis=-1, keepdims=True) + EPS)
    o_ref[...] = (y * g_ref[...]).astype(o_ref.dtype)


def _rmsnorm(x, g):
    rows, d = x.shape
    tm = _pick(rows, 640, BF16_ROWS)
    return pl.pallas_call(
        _rmsnorm_kernel,
        grid=(rows // tm,),
        in_specs=[pl.BlockSpec((tm, d), lambda i: (i, 0)), pl.BlockSpec((1, d), lambda i: (0, 0))],
        out_specs=pl.BlockSpec((tm, d), lambda i: (i, 0)),
        out_shape=jax.ShapeDtypeStruct((rows, d), BF16),
        compiler_params=_params("parallel"),
        name="rmsnorm",
    )(x, g.reshape(1, d))


def _head_rmsnorm(acc, gain):
    outs = []
    for c in range(acc.shape[1] // HEAD_DIM):
        blk = acc[:, c * HEAD_DIM:(c + 1) * HEAD_DIM]
        y = blk * lax.rsqrt(jnp.mean(blk * blk, axis=-1, keepdims=True) + EPS)
        outs.append(y * gain)
    return outs[0] if len(outs) == 1 else jnp.concatenate(outs, axis=1)


def _proj_q_kernel(xn_ref, w_ref, gain_ref, o_ref, *, nb, scale):
    acc = _dot(xn_ref[...], w_ref[...])

    @pl.when(pl.program_id(1) < nb)
    def _():
        o_ref[...] = (acc * scale).astype(o_ref.dtype)

    @pl.when(pl.program_id(1) >= nb)
    def _():
        o_ref[...] = (_head_rmsnorm(acc, gain_ref[...]) * scale).astype(o_ref.dtype)


def _project_q(xn, w_main, layer, q_gain, width):
    rows, d = xn.shape
    tm = _pick(rows, 1040, BF16_ROWS)
    tn = _pick(width, 512, LANES)
    nb = width // tn
    return pl.pallas_call(
        functools.partial(_proj_q_kernel, nb=nb, scale=HEAD_DIM ** -0.5),
        grid=(rows // tm, 2 * nb),
        in_specs=[pl.BlockSpec((tm, d), lambda i, j: (i, 0)),
                  pl.BlockSpec((None, d, tn), lambda i, j: (layer, 0, jnp.where(j < nb, j, j + 2 * nb))),
                  pl.BlockSpec((1, HEAD_DIM), lambda i, j: (0, 0))],
        out_specs=pl.BlockSpec((tm, tn), lambda i, j: (i, j)),
        out_shape=jax.ShapeDtypeStruct((rows, 2 * width), BF16),
        compiler_params=_params("parallel", "arbitrary"),
        name="proj_q",
    )(xn, w_main, q_gain.reshape(1, HEAD_DIM))


def _proj_cache_kernel(xn_ref, w_ref, gain_ref, cache_in_ref, cache_ref, ob_ref, *, normed):
    del cache_in_ref
    y = _dot(xn_ref[...], w_ref[...])
    if normed:
        y = _head_rmsnorm(y, gain_ref[...])
    ob_ref[...] = y.astype(ob_ref.dtype)
    lead = cache_ref.shape[:-2]
    for h in range(cache_ref.shape[-2]):
        piece = y[:, h * HEAD_DIM:(h + 1) * HEAD_DIM].reshape(lead + (HEAD_DIM,))
        if len(lead) == 1:
            cache_ref[:, h, :] = piece
        else:
            cache_ref[:, :, h, :] = piece


def _project_cache(xn, w_main, gain, w_section, normed, cache, layer, sample):
    rows, d = xn.shape
    _, b, t, heads, _ = cache.shape
    hb = _pick(heads, CACHE_HEADS, F32_ROWS)
    tn = hb * HEAD_DIM
    nj = heads // hb
    n = b * t
    if sample:
        tm = n
        x_spec = pl.BlockSpec((n, d), lambda i, j: ((rows - n) // n, 0))
        c_spec = pl.BlockSpec((None, b, t, hb, HEAD_DIM), lambda i, j: (layer, 0, 0, j, 0))
    else:
        assert b == 1
        tm = _pick(n, 1024, BF16_ROWS)
        x_spec = pl.BlockSpec((tm, d), lambda i, j: (i, 0))
        c_spec = pl.BlockSpec((None, None, tm, hb, HEAD_DIM), lambda i, j: (layer, 0, i, j, 0))
    return pl.pallas_call(
        functools.partial(_proj_cache_kernel, normed=normed),
        grid=(n // tm, nj),
        in_specs=[x_spec,
                  pl.BlockSpec((None, d, tn), lambda i, j: (layer, 0, w_section * nj + j)),
                  pl.BlockSpec((1, HEAD_DIM), lambda i, j: (0, 0)),
                  pl.BlockSpec(memory_space=pl.ANY)],
        out_specs=[c_spec, pl.BlockSpec((tm, tn), lambda i, j: (i, j))],
        out_shape=[jax.ShapeDtypeStruct(cache.shape, F32),
                   jax.ShapeDtypeStruct((n, heads * HEAD_DIM), BF16)],
        input_output_aliases={3: 0},
        compiler_params=_params("parallel", "arbitrary"),
        name="proj_cache",
    )(xn, w_main, gain.reshape(1, HEAD_DIM), cache)


def _logf_kernel(wf_ref, xn_ref, b_ref, o_ref):
    z = _dot_nt(wf_ref[...], xn_ref[...]) + b_ref[...]
    o_ref[...] = _log_sigmoid(z)


def _log_forget(xn, wf_t, b_forget):
    rows, d = xn.shape
    heads = wf_t.shape[0]
    tt = _pick(rows, 1664, LANES)
    return pl.pallas_call(
        _logf_kernel,
        grid=(rows // tt,),
        in_specs=[pl.BlockSpec((heads, d), lambda i: (0, 0)),
                  pl.BlockSpec((tt, d), lambda i: (i, 0)),
                  pl.BlockSpec((heads, 1), lambda i: (0, 0))],
        out_specs=pl.BlockSpec((heads, tt), lambda i: (0, i)),
        out_shape=jax.ShapeDtypeStruct((heads, rows), F32),
        compiler_params=_params("parallel"),
        name="log_forget",
    )(wf_t, xn, b_forget.reshape(heads, 1))


def _split3(x):
    hi = x.astype(BF16)
    r1 = x - hi.astype(F32)
    mid = r1.astype(BF16)
    lo = (r1 - mid.astype(F32)).astype(BF16)
    return hi, mid, lo


def _cumsum_kernel(x_ref, o_ref):
    rows, length = x_ref.shape
    r = lax.broadcasted_iota(jnp.int32, (LANES, LANES), 0)
    c = lax.broadcasted_iota(jnp.int32, (LANES, LANES), 1)
    tri = jnp.where(r <= c, 1.0, 0.0).astype(BF16)

    def body(n, carry):
        off = pl.multiple_of(n * LANES, LANES)
        hi, mid, lo = _split3(x_ref[:, pl.ds(off, LANES)])
        y = (_dot(hi, tri) + _dot(mid, tri)) + _dot(lo, tri) + carry
        o_ref[:, pl.ds(off, LANES)] = y
        return y[:, LANES - 1:LANES]

    lax.fori_loop(0, length // LANES, body, jnp.zeros((rows, 1), F32))


def _cumsum_lanes(x):
    return pl.pallas_call(
        _cumsum_kernel,
        out_shape=jax.ShapeDtypeStruct(x.shape, F32),
        compiler_params=pltpu.CompilerParams(vmem_limit_bytes=VMEM_LIMIT),
        name="cumsum",
    )(x)


ATTN_BLOCK = 1024
SUFFIX_BLOCK = 256
FOX_ROW_CHUNKS = 4


def _tri_masks(rows, cols):
    row = lax.broadcasted_iota(jnp.int32, (rows, cols), 0)
    col = lax.broadcasted_iota(jnp.int32, (rows, cols), 1)
    return row, col


def _neg_suffix(blk):
    r, c = _tri_masks(2 * blk, blk)
    r = jnp.where(r >= blk, r - blk, r)
    return jnp.where(r >= c, -1.0, 0.0).astype(BF16)


def _sb_block(q, k, v, neg_suffix, state, valid):
    acc, carry = state
    sub = neg_suffix.shape[1]
    zs, tails = [], []
    for c in range(k.shape[0] // sub):
        cols = slice(c * sub, (c + 1) * sub)
        z = _dot_nt(q, k[cols, :])
        softplus = jnp.maximum(z, 0.0) + jnp.log(1.0 + jnp.exp2(jnp.abs(z) * -LOG2E))
        if valid is not None:
            softplus = jnp.where(valid[:, cols], softplus, 0.0)
        hi = softplus.astype(BF16)
        lo = (softplus - hi.astype(F32)).astype(BF16)
        zs.append(z)
        tails.append(_dot(jnp.concatenate([hi, lo], axis=1), neg_suffix))
    for c in reversed(range(len(zs))):
        cols = slice(c * sub, (c + 1) * sub)
        tail = tails[c] + carry
        a = jnp.exp(zs[c] + tail)
        if valid is not None:
            a = jnp.where(valid[:, cols], a, 0.0)
        acc = acc + _dot(a.astype(BF16), v[cols, :])
        carry = tail[:, 0:1]
    return acc, carry


def _causal_sweep(i, blk, step, state, strict):
    row, col = _tri_masks(blk, blk)
    state = step(i, state, col < row if strict else col <= row)
    return lax.fori_loop(0, i, lambda n, st: step(i - 1 - n, st, None), state)


def _sb_prompt_kernel(q_ref, k_ref, v_ref, o_in_ref, o_ref, *, blk, sub):
    del o_in_ref
    neg_suffix = _neg_suffix(sub)
    q = q_ref[...]

    def step(j, state, valid):
        rows = pl.ds(pl.multiple_of(j * blk, blk), blk)
        return _sb_block(q, k_ref[rows, :], v_ref[rows, :], neg_suffix, state, valid)

    state = (jnp.zeros((blk, HEAD_DIM), F32), jnp.zeros((blk, 1), F32))
    acc, _ = _causal_sweep(pl.program_id(1), blk, step, state, strict=True)
    o_ref[...] = acc.astype(o_ref.dtype)


def _fox_prompt_kernel(q_ref, k_ref, v_ref, ck_ref, o_in_ref, o_ref, *, blk, chunks):
    del o_in_ref
    ones = jnp.ones((blk, HEAD_DIM), BF16)
    tq = blk // chunks

    def step(j, state, valid):
        rows = pl.ds(pl.multiple_of(j * blk, blk), blk)
        k = k_ref[rows, :]
        ck = ck_ref[:, rows]
        v1 = jnp.concatenate([v_ref[rows, :], ones], axis=1)
        new = []
        for c, (m, acc) in enumerate(state):
            s = _dot_nt(q_ref[c * tq:(c + 1) * tq, :], k) - ck
            if valid is not None:
                s = jnp.where(valid[c * tq:(c + 1) * tq], s, -jnp.inf)
            m_new = jnp.maximum(m, jnp.max(s, axis=1, keepdims=True))
            p = jnp.exp(s - m_new).astype(BF16)
            new.append((m_new, jnp.exp(m - m_new) * acc + _dot(p, v1)))
        return tuple(new)

    state = tuple((jnp.full((tq, 1), -jnp.inf, F32), jnp.zeros((tq, 2 * HEAD_DIM), F32)) for _ in range(chunks))
    state = _causal_sweep(pl.program_id(1), blk, step, state, strict=False)
    for c, (_, acc) in enumerate(state):
        o_ref[c * tq:(c + 1) * tq, :] = (acc[:, :HEAD_DIM] / acc[:, HEAD_DIM:]).astype(o_ref.dtype)


def _prompt_mixers(qq, kv, cum_k, o_a, o_b, seq, heads):
    sub = _pick(seq, SUFFIX_BLOCK, LANES)
    blk = _pick(seq, ATTN_BLOCK, sub)
    chunks = FOX_ROW_CHUNKS if blk % (FOX_ROW_CHUNKS * BF16_ROWS) == 0 else 1
    grid = (heads, seq // blk)
    q_blk = (blk, HEAD_DIM)
    kv_spec = pl.BlockSpec((seq, HEAD_DIM), lambda h, i: (0, h))
    any_spec = pl.BlockSpec(memory_space=pl.ANY)
    out_spec = pl.BlockSpec(q_blk, lambda h, i: (i, h))
    out_shape = jax.ShapeDtypeStruct(o_a.shape, BF16)

    o_a = pl.pallas_call(
        functools.partial(_sb_prompt_kernel, blk=blk, sub=sub),
        grid=grid,
        in_specs=[pl.BlockSpec(q_blk, lambda h, i: (i, h)), kv_spec, kv_spec, any_spec],
        out_specs=out_spec,
        out_shape=out_shape,
        input_output_aliases={3: 0},
        compiler_params=_params("parallel", "arbitrary"),
        name="sb_prompt",
    )(qq, kv[0], kv[1], o_a)

    o_b = pl.pallas_call(
        functools.partial(_fox_prompt_kernel, blk=blk, chunks=chunks),
        grid=grid,
        in_specs=[pl.BlockSpec(q_blk, lambda h, i: (i, heads + h)), kv_spec, kv_spec,
                  pl.BlockSpec((None, 1, seq), lambda h, i: (h, 0, 0)), any_spec],
        out_specs=out_spec,
        out_shape=out_shape,
        input_output_aliases={4: 0},
        compiler_params=_params("parallel", "arbitrary"),
        name="fox_prompt",
    )(qq, kv[2], kv[3], cum_k, o_b)
    return o_a, o_b


def _sample_kernel(qa_ref, qb_ref, kna_ref, vna_ref, knb_ref, vnb_ref,
                   kca_ref, vca_ref, kcb_ref, vcb_ref, cum_ref, oa_in_ref, ob_in_ref, oa_ref, ob_ref, *, past, tn):
    del oa_in_ref, ob_in_ref
    pad = jnp.zeros((LANES - tn, HEAD_DIM), BF16)
    row, col = _tri_masks(tn, LANES)
    neg_suffix = _neg_suffix(LANES)
    n_chunks = past // LANES

    q = qa_ref[...]
    k_new = jnp.concatenate([kna_ref[...], pad], axis=0)
    v_new = jnp.concatenate([vna_ref[...], pad], axis=0)
    state = (jnp.zeros((tn, HEAD_DIM), F32), jnp.zeros((tn, 1), F32))
    state = _sb_block(q, k_new, v_new, neg_suffix, state, col < row)
    head = pl.program_id(1) % kca_ref.shape[1]
    for c in reversed(range(n_chunks)):
        k = kca_ref[c * LANES:(c + 1) * LANES, head, :].astype(BF16)
        v = vca_ref[c * LANES:(c + 1) * LANES, head, :].astype(BF16)
        state = _sb_block(q, k, v, neg_suffix, state, None)
    oa_ref[...] = state[0].astype(oa_ref.dtype)

    q = qb_ref[...]
    k_new = jnp.concatenate([knb_ref[...], pad], axis=0)
    v_new = jnp.concatenate([vnb_ref[...], pad], axis=0)
    s_new = _dot_nt(q, k_new) - cum_ref[:, past:past + LANES]
    s_new = jnp.where(col <= row, s_new, -jnp.inf)
    s_old = _dot_nt(q, kcb_ref[:, head, :].astype(BF16)) - cum_ref[:, 0:past]
    m = jnp.maximum(jnp.max(s_new, axis=1, keepdims=True), jnp.max(s_old, axis=1, keepdims=True))
    p_new = jnp.exp(s_new - m)
    p_old = jnp.exp(s_old - m)
    l = jnp.sum(p_new, axis=1, keepdims=True) + jnp.sum(p_old, axis=1, keepdims=True)
    acc = _dot(p_new.astype(BF16), v_new) + _dot(p_old.astype(BF16), vcb_ref[:, head, :].astype(BF16))
    ob_ref[...] = (acc / l).astype(ob_ref.dtype)


def _sample_mixers(qq, kv_new, caches, layer, cum_s, o_a, o_b, heads, n_streams, tn):
    past = caches[0].shape[2]
    row0 = o_a.shape[0] // tn - n_streams
    new_blk = (tn, HEAD_DIM)
    resident = _pick(heads, CACHE_HEADS, F32_ROWS)
    q_specs = [pl.BlockSpec(new_blk, lambda b, h: (row0 + b, h)),
               pl.BlockSpec(new_blk, lambda b, h: (row0 + b, heads + h))]
    new_spec = pl.BlockSpec(new_blk, lambda b, h: (b, h))
    cache_spec = pl.BlockSpec((None, None, past, resident, HEAD_DIM), lambda b, h: (layer, b, 0, h // resident, 0))
    any_spec = pl.BlockSpec(memory_space=pl.ANY)
    out_spec = pl.BlockSpec(new_blk, lambda b, h: (row0 + b, h))
    out_shape = jax.ShapeDtypeStruct(o_a.shape, BF16)
    return pl.pallas_call(
        functools.partial(_sample_kernel, past=past, tn=tn),
        grid=(n_streams, heads),
        in_specs=q_specs + [new_spec] * 4 + [cache_spec] * 4
                 + [pl.BlockSpec((None, None, 1, past + LANES), lambda b, h: (b, h, 0, 0)), any_spec, any_spec],
        out_specs=[out_spec, out_spec],
        out_shape=[out_shape, out_shape],
        input_output_aliases={11: 0, 12: 1},
        compiler_params=_params("parallel", "arbitrary"),
        name="sample_mixers",
    )(qq, qq, *kv_new, *caches, cum_s, o_a, o_b)


def _mix_kernel(oa_ref, ob_ref, xn_ref, wpa_ref, wpb_ref, wga_ref, wgb_ref, ba_ref, bb_ref, o_ref):
    xn = xn_ref[...]
    g_a = jax.nn.sigmoid(_dot(xn, wga_ref[...]) + ba_ref[...])
    g_b = jax.nn.sigmoid(_dot(xn, wgb_ref[...]) + bb_ref[...])
    mix = g_a * _dot(oa_ref[...], wpa_ref[...]) + g_b * _dot(ob_ref[...], wpb_ref[...])
    o_ref[...] = mix.astype(o_ref.dtype)


def _gated_mix(o_a, o_b, xn, w_proj_a, w_proj_b, w_gate, layer, b_gate):
    rows, d = xn.shape
    width = o_a.shape[1]
    tm = _pick(rows, 832, BF16_ROWS)
    tn = _pick(d, 256, LANES)
    nb = d // tn
    b2 = b_gate.reshape(1, 2 * d)
    return pl.pallas_call(
        _mix_kernel,
        grid=(rows // tm, nb),
        in_specs=[pl.BlockSpec((tm, width), lambda i, j: (i, 0)),
                  pl.BlockSpec((tm, width), lambda i, j: (i, 0)),
                  pl.BlockSpec((tm, d), lambda i, j: (i, 0)),
                  pl.BlockSpec((None, width, tn), lambda i, j: (layer, 0, j)),
                  pl.BlockSpec((None, width, tn), lambda i, j: (layer, 0, j)),
                  pl.BlockSpec((None, d, tn), lambda i, j: (layer, 0, j)),
                  pl.BlockSpec((None, d, tn), lambda i, j: (layer, 0, nb + j)),
                  pl.BlockSpec((1, tn), lambda i, j: (0, j)),
                  pl.BlockSpec((1, tn), lambda i, j: (0, nb + j))],
        out_specs=pl.BlockSpec((tm, tn), lambda i, j: (i, j)),
        out_shape=jax.ShapeDtypeStruct((rows, d), BF16),
        compiler_params=_params("parallel", "arbitrary"),
        name="gated_mix",
    )(o_a, o_b, xn, w_proj_a, w_proj_b, w_gate, w_gate, b2, b2)


def _residual_matmul_kernel(a_ref, w_ref, r_ref, o_ref):
    o_ref[...] = r_ref[...] + _dot(a_ref[...], w_ref[...])


def _residual_matmul(a, w, layer, resid, tm_target, tn_target):
    rows, k = a.shape
    n = w.shape[2]
    tm = _pick(rows, tm_target, BF16_ROWS)
    tn = _pick(n, tn_target, LANES)
    return pl.pallas_call(
        _residual_matmul_kernel,
        grid=(rows // tm, n // tn),
        in_specs=[pl.BlockSpec((tm, k), lambda i, j: (i, 0)),
                  pl.BlockSpec((None, k, tn), lambda i, j: (layer, 0, j)),
                  pl.BlockSpec((tm, tn), lambda i, j: (i, j))],
        out_specs=pl.BlockSpec((tm, tn), lambda i, j: (i, j)),
        out_shape=jax.ShapeDtypeStruct((rows, n), F32),
        compiler_params=_params("parallel", "arbitrary"),
        name="residual_matmul",
    )(a, w, resid)


def _swiglu_up_kernel(x_ref, wg_ref, wu_ref, o_ref):
    x = x_ref[...]
    g = _dot(x, wg_ref[...])
    o_ref[...] = (g * jax.nn.sigmoid(g) * _dot(x, wu_ref[...])).astype(o_ref.dtype)


def _swiglu_up(hn, w_gate, w_up, layer):
    rows, d = hn.shape
    dff = w_gate.shape[2]
    tm = _pick(rows, 1040, BF16_ROWS)
    tn = _pick(dff, 256, LANES)
    return pl.pallas_call(
        _swiglu_up_kernel,
        grid=(rows // tm, dff // tn),
        in_specs=[pl.BlockSpec((tm, d), lambda i, j: (i, 0)),
                  pl.BlockSpec((None, d, tn), lambda i, j: (layer, 0, j)),
                  pl.BlockSpec((None, d, tn), lambda i, j: (layer, 0, j))],
        out_specs=pl.BlockSpec((tm, tn), lambda i, j: (i, j)),
        out_shape=jax.ShapeDtypeStruct((rows, dff), BF16),
        compiler_params=_params("parallel", "arbitrary"),
        name="swiglu_up",
    )(hn, w_gate, w_up)


def kernel(x_prompt, x_sample, cache_sb_k, cache_sb_v, cache_fox_k, cache_fox_v, cache_fox_logf, norm_attn, w_in, b_forget, q_norm, k_norm, w_proj_a, w_proj_b, w_gate_br, b_gate_br, w_out, norm_ffn, w_ffn_gate, w_ffn_up, w_ffn_down):
    batch, seq, d = x_prompt.shape
    n_streams, tn, _ = x_sample.shape
    depth, _, past, heads, _ = cache_sb_k.shape
    assert batch == 1 and seq % tn == 0 and tn % BF16_ROWS == 0 and past % LANES == 0
    width = heads * HEAD_DIM
    n_new = n_streams * tn
    rows = seq + n_new

    x = jnp.concatenate([x_prompt.reshape(seq, d), x_sample.reshape(n_new, d)], axis=0)
    caches = [cache_sb_k, cache_sb_v, cache_fox_k, cache_fox_v]
    logf_past = jnp.transpose(cache_fox_logf, (0, 1, 3, 2)).astype(F32)

    p_cache = [jnp.zeros((depth, batch, seq, heads, HEAD_DIM), F32) for _ in range(4)]
    s_cache = [jnp.zeros((depth, n_streams, tn, heads, HEAD_DIM), F32) for _ in range(4)]
    o_a = jnp.zeros((rows, width), BF16)
    o_b = jnp.zeros((rows, width), BF16)
    p_logf, s_logf = [], []
    kv_sections = ((1, False), (2, False), (4, True), (5, False))
    w_main = w_in.astype(BF16)
    wpa, wpb, wgb, wo = (w.astype(BF16) for w in (w_proj_a, w_proj_b, w_gate_br, w_out))
    wfg, wfu, wfd = (w.astype(BF16) for w in (w_ffn_gate, w_ffn_up, w_ffn_down))
    for l in range(depth):
        wf_t = jnp.transpose(w_in[l, :, 6 * width:]).astype(BF16)

        xn = _rmsnorm(x, norm_attn[l])
        qq = _project_q(xn, w_main, l, q_norm[l], width)
        kv_p, kv_s = [], []
        for n, (section, normed) in enumerate(kv_sections):
            p_cache[n], kb = _project_cache(xn, w_main, k_norm[l], section, normed, p_cache[n], l, sample=False)
            s_cache[n], ks = _project_cache(xn, w_main, k_norm[l], section, normed, s_cache[n], l, sample=True)
            kv_p.append(kb)
            kv_s.append(ks)
        logf_t = _log_forget(xn, wf_t, b_forget[l])

        cum_p = _cumsum_lanes(logf_t[:, :seq]).reshape(heads, 1, seq)
        logf_new = jnp.transpose(logf_t[:, seq:].reshape(heads, n_streams, tn), (1, 0, 2))
        logf_all = jnp.concatenate(
            [logf_past[l], logf_new, jnp.zeros((n_streams, heads, LANES - tn), F32)], axis=2)
        cum_s = _cumsum_lanes(logf_all.reshape(n_streams * heads, past + LANES))
        cum_s = cum_s.reshape(n_streams, heads, 1, past + LANES)

        o_a, o_b = _sample_mixers(qq, kv_s, caches, l, cum_s, o_a, o_b, heads, n_streams, tn)
        o_a, o_b = _prompt_mixers(qq, kv_p, cum_p, o_a, o_b, seq, heads)

        mix = _gated_mix(o_a, o_b, xn, wpa, wpb, wgb, l, b_gate_br[l])
        h = _residual_matmul(mix, wo, l, x, 1040, 512)
        hn = _rmsnorm(h, norm_ffn[l])
        act = _swiglu_up(hn, wfg, wfu, l)
        x = _residual_matmul(act, wfd, l, h, 640, 256)

        logf_rows = jnp.transpose(logf_t)
        p_logf.append(logf_rows[:seq].reshape(batch, seq, heads))
        s_logf.append(logf_rows[seq:].reshape(n_streams, tn, heads))

    return (x[:seq].reshape(batch, seq, d), x[seq:].reshape(n_streams, tn, d),
            *p_cache, jnp.stack(p_logf), *s_cache, jnp.stack(s_logf))
```
